```python
import math
import jax
import jax.numpy as jnp
from jax import lax
import numpy as np

D_MODEL = 2048
BATCH = 1
SEQ = 8192
DEPTH = 1
DEC_BATCH = 128
DEC_SEQ = 1
PAST_LEN = 16384
PAGE_SIZE = 128

D_RNN = 1024
RG_BLOCKS = 8
RG_BLOCK = D_RNN // RG_BLOCKS
CONV_W = 4
RG_C = 8.0
N_HEADS = 16
QK_NOPE = 128
QK_ROPE = 64
V_DIM = 128
KV_RANK = 512
ROPE_THETA = 10000.0
Q_BLOCK = 128
N_MEM = 256
MEM_HEADS = 4
MEM_DIM = 256
EPS = 1e-6

kernel_name = 'hawk_mla_memory_hybrid_step'


def split_sizes():
    return [D_RNN, D_RNN, N_HEADS * (QK_NOPE + QK_ROPE), KV_RANK, QK_ROPE, N_HEADS * V_DIM,
            MEM_HEADS * MEM_DIM, MEM_HEADS * MEM_DIM, D_MODEL, D_MODEL, D_MODEL]


def rmsnorm(x, g):
    x32 = x.astype(jnp.float32)
    y = x32 * lax.rsqrt(jnp.mean(x32 * x32, axis=-1, keepdims=True) + EPS)
    return (y * g.astype(jnp.float32)).astype(x.dtype)


def project_in(x, norm_in, w_in):
    z = rmsnorm(x, norm_in) @ w_in
    idx, acc = [], 0
    for s in split_sizes()[:-1]:
        acc += s
        idx.append(acc)
    return jnp.split(z, idx, axis=-1)


def rope_tables(pos, dtype):
    inv = ROPE_THETA ** (-jnp.arange(0, QK_ROPE, 2, dtype=jnp.float32) / QK_ROPE)
    ang = pos.astype(jnp.float32)[:, None] * inv[None, :]
    return jnp.cos(ang).astype(dtype), jnp.sin(ang).astype(dtype)


def apply_rope(x, cos, sin):
    x1, x2 = jnp.split(x, 2, axis=-1)
    return jnp.concatenate([x1 * cos - x2 * sin, x2 * cos + x1 * sin], axis=-1)


def causal_conv(x, prev, w, b):
    L = x.shape[1]
    xp = jnp.concatenate([prev.astype(x.dtype), x], axis=1)
    out = b + xp[:, 0:L] * w[0]
    for k in range(1, CONV_W):
        out = out + xp[:, k:k + L] * w[k]
    return out, xp[:, -(CONV_W - 1):]


def rg_branch(xr, prev_conv, h0, conv_w, conv_b, rg_wa, rg_ba, rg_wi, rg_bi, rg_lambda):
    B, L, _ = xr.shape
    xc, new_conv = causal_conv(xr, prev_conv, conv_w, conv_b)
    xb = xc.reshape(B, L, RG_BLOCKS, RG_BLOCK)
    r = jax.nn.sigmoid((jnp.einsum('blnc,ncd->blnd', xb, rg_wa).reshape(B, L, D_RNN) + rg_ba).astype(jnp.float32))
    i = jax.nn.sigmoid((jnp.einsum('blnc,ncd->blnd', xb, rg_wi).reshape(B, L, D_RNN) + rg_bi).astype(jnp.float32))
    log_a = -RG_C * r * jax.nn.softplus(-rg_lambda.astype(jnp.float32))
    a = jnp.exp(log_a)
    u = jnp.sqrt(-jnp.expm1(2.0 * log_a)) * (i * xc.astype(jnp.float32))

    def step(h, au):
        a_t, u_t = au
        h = a_t * h + u_t
        return h, h

    hT, hs = lax.scan(step, h0.astype(jnp.float32), (a.transpose(1, 0, 2), u.transpose(1, 0, 2)))
    return hs.transpose(1, 0, 2).astype(xr.dtype), new_conv, hT


def mla_common(q, ckv_raw, kr_raw, pos, kv_norm):
    B, L, _ = q.shape
    q = q.reshape(B, L, N_HEADS, QK_NOPE + QK_ROPE)
    q_nope, q_rope = q[..., :QK_NOPE], q[..., QK_NOPE:]
    cos, sin = rope_tables(pos, q.dtype)
    q_rope = apply_rope(q_rope, cos[:, None, :], sin[:, None, :])
    ckv = rmsnorm(ckv_raw, kv_norm)
    k_rope = apply_rope(kr_raw, cos, sin)
    return q_nope, q_rope, ckv, k_rope


def mla_prompt(q, ckv_raw, kr_raw, kv_norm, w_uk, w_uv):
    B, L, _ = q.shape
    pos = jnp.arange(L)
    q_nope, q_rope, ckv, k_rope = mla_common(q, ckv_raw, kr_raw, pos, kv_norm)
    k_nope = jnp.einsum('bsc,chd->bshd', ckv, w_uk)
    v = jnp.einsum('bsc,chd->bshd', ckv, w_uv)
    scale = (QK_NOPE + QK_ROPE) ** -0.5
    nb = L // Q_BLOCK
    qn_b = q_nope.reshape(B, nb, Q_BLOCK, N_HEADS, QK_NOPE).transpose(1, 0, 2, 3, 4)
    qr_b = q_rope.reshape(B, nb, Q_BLOCK, N_HEADS, QK_ROPE).transpose(1, 0, 2, 3, 4)

    def block(args):
        qn, qr, b_idx = args
        s = jnp.einsum('bqhd,bkhd->bhqk', qn, k_nope) + jnp.einsum('bqhd,bkd->bhqk', qr, k_rope)
        s = s.astype(jnp.float32) * scale
        qpos = b_idx * Q_BLOCK + jnp.arange(Q_BLOCK)
        s = jnp.where(pos[None, :] <= qpos[:, None], s, -jnp.inf)
        p = jax.nn.softmax(s, axis=-1).astype(v.dtype)
        return jnp.einsum('bhqk,bkhd->bqhd', p, v)

    o = lax.map(block, (qn_b, qr_b, jnp.arange(nb)))
    o = o.transpose(1, 0, 2, 3, 4).reshape(B, L, N_HEADS * V_DIM)
    return o, ckv, k_rope


def mla_sample(q, ckv_raw, kr_raw, cache_ckv, cache_krope, page_table, kv_norm, w_uk, w_uv):
    Bd, T, _ = q.shape
    pos = PAST_LEN + jnp.arange(T)
    q_nope, q_rope, ckv, k_rope = mla_common(q, ckv_raw, kr_raw, pos, kv_norm)
    q_lat = jnp.einsum('bthd,chd->bthc', q_nope, w_uk)
    scale = (QK_NOPE + QK_ROPE) ** -0.5
    tmask = jnp.arange(T)[None, :] <= jnp.arange(T)[:, None]

    def one_seq(args):
        pt, ql, qr, cn, kn = args
        c_past = cache_ckv[pt].reshape(-1, KV_RANK).astype(ql.dtype)
        r_past = cache_krope[pt].reshape(-1, QK_ROPE).astype(ql.dtype)
        s_past = jnp.einsum('thc,kc->htk', ql, c_past) + jnp.einsum('thr,kr->htk', qr, r_past)
        s_new = jnp.einsum('thc,jc->htj', ql, cn) + jnp.einsum('thr,jr->htj', qr, kn)
        s_new = jnp.where(tmask[None], s_new.astype(jnp.float32) * scale, -jnp.inf)
        s = jnp.concatenate([s_past.astype(jnp.float32) * scale, s_new], axis=-1)
        p = jax.nn.softmax(s, axis=-1).astype(ql.dtype)
        n_past = c_past.shape[0]
        return jnp.einsum('htk,kc->thc', p[..., :n_past], c_past) + jnp.einsum('htj,jc->thc', p[..., n_past:], cn)

    lat = lax.map(one_seq, (page_table, q_lat, q_rope, ckv, k_rope))
    o = jnp.einsum('bthc,chd->bthd', lat, w_uv).reshape(Bd, T, N_HEADS * V_DIM)
    return o, ckv, k_rope


def mem_kv(mem, mem_norm, w_mk, w_mv):
    B = mem.shape[0]
    m = rmsnorm(mem, mem_norm)
    k = (m @ w_mk).reshape(B, N_MEM, MEM_HEADS, MEM_DIM)
    v = (m @ w_mv).reshape(B, N_MEM, MEM_HEADS, MEM_DIM)
    return k, v


def mem_attend(q, k, v):
    B, L, _ = q.shape
    q = q.reshape(B, L, MEM_HEADS, MEM_DIM)
    s = jnp.einsum('blhd,bmhd->bhlm', q, k.astype(q.dtype)).astype(jnp.float32) * (MEM_DIM ** -0.5)
    p = jax.nn.softmax(s, axis=-1).astype(q.dtype)
    return jnp.einsum('bhlm,bmhd->blhd', p, v.astype(q.dtype)).reshape(B, L, MEM_HEADS * MEM_DIM)


def merge(x, y_rg, g_rg, o_mla, g_mla, o_mem, g_mem, m_rg, m_mla, m_mem,
          w_rg_o, w_mla_o, w_mem_o, w_out, final_norm):
    sg, si = jax.nn.sigmoid, jax.nn.silu
    z = (sg(m_rg) * ((y_rg * si(g_rg)) @ w_rg_o)
         + sg(m_mla) * ((o_mla * si(g_mla)) @ w_mla_o)
         + sg(m_mem) * ((o_mem * si(g_mem)) @ w_mem_o))
    return rmsnorm(x + z @ w_out, final_norm)


def setup_inputs(seed: int = 0) -> dict:
    key = jax.random.key(seed)
    ks = jax.random.split(key, 32)
    f32 = jnp.float32
    n_pages = PAST_LEN // PAGE_SIZE
    n_used = DEC_BATCH * n_pages
    n_pool = n_used + max(1, n_used // 4)

    def nrm(k, shape, scale):
        return jax.random.normal(k, shape, f32) * scale

    def gain(k, n):
        return 1.0 + 0.01 * jax.random.normal(k, (n,), f32)

    page_table = jax.random.permutation(ks[9], n_pool)[:n_used].reshape(DEC_BATCH, n_pages).astype(jnp.int32)
    rad = jnp.sqrt(jax.random.uniform(ks[17], (D_RNN,), f32, 0.81, 0.998))
    rg_lambda = jnp.log(rad) - jnp.log1p(-rad)
    total_in = sum(split_sizes())
    return {
        'x_prompt': nrm(ks[0], (BATCH, SEQ, D_MODEL), 1.0),
        'x_sample': nrm(ks[1], (DEC_BATCH, DEC_SEQ, D_MODEL), 1.0),
        'mem_prompt': nrm(ks[2], (BATCH, N_MEM, D_MODEL), 1.0),
        'cache_ckv': nrm(ks[3], (n_pool, PAGE_SIZE, KV_RANK), 1.0),
        'cache_krope': nrm(ks[4], (n_pool, PAGE_SIZE, QK_ROPE), 1.0),
        'cache_mem_k': nrm(ks[5], (DEC_BATCH, N_MEM, MEM_HEADS, MEM_DIM), 1.0),
        'cache_mem_v': nrm(ks[6], (DEC_BATCH, N_MEM, MEM_HEADS, MEM_DIM), 1.0),
        'state_conv': nrm(ks[7], (DEC_BATCH, CONV_W - 1, D_RNN), 1.0),
        'state_rglru': nrm(ks[8], (DEC_BATCH, D_RNN), 0.5),
        'page_table': page_table,
        'norm_in': gain(ks[10], D_MODEL),
        'w_in': nrm(ks[11], (D_MODEL, total_in), D_MODEL ** -0.5),
        'conv_w': nrm(ks[12], (CONV_W, D_RNN), CONV_W ** -0.5),
        'conv_b': nrm(ks[13], (D_RNN,), 0.01),
        'rg_wa': nrm(ks[14], (RG_BLOCKS, RG_BLOCK, RG_BLOCK), RG_BLOCK ** -0.5),
        'rg_ba': nrm(ks[15], (D_RNN,), 0.01),
        'rg_wi': nrm(ks[16], (RG_BLOCKS, RG_BLOCK, RG_BLOCK), RG_BLOCK ** -0.5),
        'rg_bi': nrm(ks[18], (D_RNN,), 0.01),
        'rg_lambda': rg_lambda,
        'kv_norm': gain(ks[19], KV_RANK),
        'w_uk': nrm(ks[20], (KV_RANK, N_HEADS, QK_NOPE), KV_RANK ** -0.5),
        'w_uv': nrm(ks[21], (KV_RANK, N_HEADS, V_DIM), KV_RANK ** -0.5),
        'mem_norm': gain(ks[22], D_MODEL),
        'w_mk': nrm(ks[23], (D_MODEL, MEM_HEADS * MEM_DIM), D_MODEL ** -0.5),
        'w_mv': nrm(ks[24], (D_MODEL, MEM_HEADS * MEM_DIM), D_MODEL ** -0.5),
        'w_rg_o': nrm(ks[25], (D_RNN, D_MODEL), D_RNN ** -0.5),
        'w_mla_o': nrm(ks[26], (N_HEADS * V_DIM, D_MODEL), (N_HEADS * V_DIM) ** -0.5),
        'w_mem_o': nrm(ks[27], (MEM_HEADS * MEM_DIM, D_MODEL), (MEM_HEADS * MEM_DIM) ** -0.5),
        'w_out': nrm(ks[28], (D_MODEL, D_MODEL), D_MODEL ** -0.5),
        'final_norm': gain(ks[29], D_MODEL),
    }


def reference(x_prompt, x_sample, mem_prompt, cache_ckv, cache_krope, cache_mem_k, cache_mem_v,
              state_conv, state_rglru, page_table, norm_in, w_in, conv_w, conv_b, rg_wa, rg_ba,
              rg_wi, rg_bi, rg_lambda, kv_norm, w_uk, w_uv, mem_norm, w_mk, w_mv, w_rg_o, w_mla_o,
              w_mem_o, w_out, final_norm):
    y_prompt, y_sample = x_prompt, x_sample
    for _ in range(DEPTH):
        xr, g_rg, q, ckv_raw, kr_raw, g_mla, q_mem, g_mem, m_rg, m_mla, m_mem = project_in(y_prompt, norm_in, w_in)
        Bp = xr.shape[0]
        y_rg, new_conv_prompt, new_rglru_prompt = rg_branch(
            xr, jnp.zeros((Bp, CONV_W - 1, D_RNN), xr.dtype), jnp.zeros((Bp, D_RNN), jnp.float32),
            conv_w, conv_b, rg_wa, rg_ba, rg_wi, rg_bi, rg_lambda)
        o_mla, new_ckv_prompt, new_krope_prompt = mla_prompt(q, ckv_raw, kr_raw, kv_norm, w_uk, w_uv)
        new_mem_k_prompt, new_mem_v_prompt = mem_kv(mem_prompt, mem_norm, w_mk, w_mv)
        o_mem = mem_attend(q_mem, new_mem_k_prompt, new_mem_v_prompt)
        y_prompt = merge(y_prompt, y_rg, g_rg, o_mla, g_mla, o_mem, g_mem, m_rg, m_mla, m_mem,
                         w_rg_o, w_mla_o, w_mem_o, w_out, final_norm)

        xr, g_rg, q, ckv_raw, kr_raw, g_mla, q_mem, g_mem, m_rg, m_mla, m_mem = project_in(y_sample, norm_in, w_in)
        y_rg, new_conv_sample, new_rglru_sample = rg_branch(
            xr, state_conv, state_rglru, conv_w, conv_b, rg_wa, rg_ba, rg_wi, rg_bi, rg_lambda)
        o_mla, new_ckv_sample, new_krope_sample = mla_sample(
            q, ckv_raw, kr_raw, cache_ckv, cache_krope, page_table, kv_norm, w_uk, w_uv)
        o_mem = mem_attend(q_mem, cache_mem_k, cache_mem_v)
        y_sample = merge(y_sample, y_rg, g_rg, o_mla, g_mla, o_mem, g_mem, m_rg, m_mla, m_mem,
                         w_rg_o, w_mla_o, w_mem_o, w_out, final_norm)
    return (y_prompt, y_sample, new_ckv_prompt, new_krope_prompt, new_conv_prompt, new_rglru_prompt,
            new_mem_k_prompt, new_mem_v_prompt, new_ckv_sample, new_krope_sample, new_conv_sample,
            new_rglru_sample)
```

```python
import functools
import math

import jax
import jax.numpy as jnp
from jax import lax
from jax.experimental import pallas as pl
from jax.experimental.pallas import tpu as pltpu

F32 = jnp.float32
BF = jnp.bfloat16

EPS = 1e-6
RG_BLOCKS = 8
RG_BLOCK = 128
CONV_W = 4
RG_C = 8.0
N_HEADS = 16
QK_NOPE = 128
QK_ROPE = 64
V_DIM = 128
KV_RANK = 512
ROPE_THETA = 10000.0
MEM_HEADS = 4
MEM_DIM = 256
PAGE_SIZE = 128
LANES = 128
HEAD_PAD = 256
LAT_PAD = KV_RANK + LANES
VMEM_LIMIT = 48 * 1024 * 1024

Z_XR, Z_GRG, Z_GMLA, Z_QMEM, Z_GMEM, Z_MRG, Z_MMLA, Z_MMEM, Z_COLS = (
    0, 1024, 2048, 4096, 5120, 6144, 8192, 10240, 12288)


def _cp(*sem):
    return pltpu.CompilerParams(dimension_semantics=sem, vmem_limit_bytes=VMEM_LIMIT)


def _silu(g):
    return g * jax.nn.sigmoid(g)


def _rms(x, g):
    return x * lax.rsqrt(jnp.mean(x * x, axis=-1, keepdims=True) + EPS) * g


def _dot(a, b):
    return jnp.dot(a, b, preferred_element_type=F32)


def _dot_t(a, b):
    return lax.dot_general(a, b, (((1,), (1,)), ((), ())), preferred_element_type=F32)


def _norm_cast_kernel(x_ref, g_ref, o_ref):
    o_ref[...] = _rms(x_ref[...], g_ref[...]).astype(o_ref.dtype)


def _norm_cast(x, g, tm):
    m, d = x.shape
    return pl.pallas_call(
        _norm_cast_kernel,
        grid=(m // tm,),
        in_specs=[pl.BlockSpec((tm, d), lambda i: (i, 0)),
                  pl.BlockSpec((1, d), lambda i: (0, 0))],
        out_specs=pl.BlockSpec((tm, d), lambda i: (i, 0)),
        out_shape=jax.ShapeDtypeStruct((m, d), BF),
        compiler_params=_cp("parallel"),
        name="norm_cast",
    )(x, g.reshape(1, d))


def _mm_kernel(a_ref, w_ref, o_ref):
    o_ref[...] = _dot(a_ref[...], w_ref[...]).astype(o_ref.dtype)


def _matmul(a, w, tm, tn, out_dtype, name):
    m, k = a.shape
    n = w.shape[1]
    return pl.pallas_call(
        _mm_kernel,
        grid=(m // tm, n // tn),
        in_specs=[pl.BlockSpec((tm, k), lambda i, j: (i, 0)),
                  pl.BlockSpec((k, tn), lambda i, j: (0, j))],
        out_specs=pl.BlockSpec((tm, tn), lambda i, j: (i, j)),
        out_shape=jax.ShapeDtypeStruct((m, n), out_dtype),
        compiler_params=_cp("parallel", "parallel"),
        name=name,
    )(a, w)


def _rope_hi(hi, c, s):
    return hi * c + pltpu.roll(hi, QK_ROPE, 1) * s


def _qproj_kernel(a_ref, w_ref, c_ref, s_ref, o_ref):
    res = _dot(a_ref[...], w_ref[0])
    o_ref[0, :, :QK_NOPE] = res[:, :QK_NOPE].astype(BF)
    o_ref[0, :, QK_NOPE:] = _rope_hi(res[:, QK_NOPE:], c_ref[...], s_ref[...]).astype(BF)


def _qproj(xn, wq, cos_t, sin_t, tm):
    m, d = xn.shape
    return pl.pallas_call(
        _qproj_kernel,
        grid=(m // tm, N_HEADS),
        in_specs=[pl.BlockSpec((tm, d), lambda i, h: (i, 0)),
                  pl.BlockSpec((1, d, HEAD_PAD), lambda i, h: (h, 0, 0)),
                  pl.BlockSpec((tm, LANES), lambda i, h: (i, 0)),
                  pl.BlockSpec((tm, LANES), lambda i, h: (i, 0))],
        out_specs=pl.BlockSpec((1, tm, HEAD_PAD), lambda i, h: (h, i, 0)),
        out_shape=jax.ShapeDtypeStruct((N_HEADS, m, HEAD_PAD), BF),
        compiler_params=_cp("parallel", "parallel"),
        name="qproj",
    )(xn, wq, cos_t, sin_t)


def _kvproj_kernel(a_ref, w_ref, c_ref, s_ref, g_ref, ckv_ref, kr_ref, ckvb_ref, krb_ref):
    res = _dot(a_ref[...], w_ref[...])
    ckv = _rms(res[:, :KV_RANK], g_ref[...])
    ckv_ref[...] = ckv
    ckvb_ref[...] = ckv.astype(BF)
    rot = _rope_hi(res[:, KV_RANK:], c_ref[...], s_ref[...])
    kr_ref[...] = rot[:, :QK_ROPE]
    krb_ref[...] = rot.astype(BF)


def _kvproj(xn, wkv, cos_t, sin_t, kv_norm, tm):
    m, d = xn.shape
    n = wkv.shape[1]
    row = lambda i: (i, 0)
    fix = lambda i: (0, 0)
    return pl.pallas_call(
        _kvproj_kernel,
        grid=(m // tm,),
        in_specs=[pl.BlockSpec((tm, d), row), pl.BlockSpec((d, n), fix),
                  pl.BlockSpec((tm, LANES), row), pl.BlockSpec((tm, LANES), row),
                  pl.BlockSpec((1, KV_RANK), fix)],
        out_specs=[pl.BlockSpec((tm, KV_RANK), row), pl.BlockSpec((tm, QK_ROPE), row),
                   pl.BlockSpec((tm, KV_RANK), row), pl.BlockSpec((tm, LANES), row)],
        out_shape=[jax.ShapeDtypeStruct((m, KV_RANK), F32), jax.ShapeDtypeStruct((m, QK_ROPE), F32),
                   jax.ShapeDtypeStruct((m, KV_RANK), BF), jax.ShapeDtypeStruct((m, LANES), BF)],
        compiler_params=_cp("parallel"),
        name="kvproj",
    )(xn, wkv, cos_t, sin_t, kv_norm.reshape(1, KV_RANK))


def _kvup_kernel(c_ref, kr_ref, wk_ref, wv_ref, k_ref, v_ref):
    c = c_ref[...]
    kn = _dot(c, wk_ref[...])
    v = _dot(c, wv_ref[...])
    kr = kr_ref[...]
    for h in range(N_HEADS):
        k_ref[h, :, :QK_NOPE] = kn[:, h * QK_NOPE:(h + 1) * QK_NOPE].astype(BF)
        k_ref[h, :, QK_NOPE:] = kr
        v_ref[h] = v[:, h * V_DIM:(h + 1) * V_DIM].astype(BF)


def _kvup(ckvb, krb, wuk, wuv, tm):
    s = ckvb.shape[0]
    row = lambda i: (i, 0)
    fix = lambda i: (0, 0)
    return pl.pallas_call(
        _kvup_kernel,
        grid=(s // tm,),
        in_specs=[pl.BlockSpec((tm, KV_RANK), row), pl.BlockSpec((tm, LANES), row),
                  pl.BlockSpec(wuk.shape, fix), pl.BlockSpec(wuv.shape, fix)],
        out_specs=[pl.BlockSpec((N_HEADS, tm, HEAD_PAD), lambda i: (0, i, 0)),
                   pl.BlockSpec((N_HEADS, tm, V_DIM), lambda i: (0, i, 0))],
        out_shape=[jax.ShapeDtypeStruct((N_HEADS, s, HEAD_PAD), BF),
                   jax.ShapeDtypeStruct((N_HEADS, s, V_DIM), BF)],
        compiler_params=_cp("parallel"),
        name="kvup",
    )(ckvb, krb, wuk, wuv)


def _flash_kernel(q_ref, k_ref, v_ref, g_ref, o_ref, m_sc, l_sc, acc_sc, *, tq, scale):
    qi = pl.program_id(1)
    q = q_ref[0]
    m_sc[...] = jnp.full(m_sc.shape, -jnp.inf, F32)
    l_sc[...] = jnp.zeros(l_sc.shape, F32)
    acc_sc[...] = jnp.zeros(acc_sc.shape, F32)

    def step(kj, masked):
        start = pl.multiple_of(kj * tq, tq)
        s = _dot_t(q, k_ref[0, pl.ds(start, tq), :]) * scale
        if masked:
            row = lax.broadcasted_iota(jnp.int32, (tq, tq), 0)
            col = lax.broadcasted_iota(jnp.int32, (tq, tq), 1)
            s = jnp.where(col <= row, s, -jnp.inf)
        m_prev = m_sc[...]
        m_new = jnp.maximum(m_prev, jnp.max(s, axis=1, keepdims=True))
        alpha = jnp.exp(m_prev - m_new)
        p = jnp.exp(s - m_new)
        l_sc[...] = alpha * l_sc[...] + jnp.sum(p, axis=1, keepdims=True)
        acc_sc[...] = alpha * acc_sc[...] + _dot(p.astype(BF), v_ref[0, pl.ds(start, tq), :])
        m_sc[...] = m_new

    def body(kj, carry):
        step(kj, False)
        return carry

    lax.fori_loop(0, qi, body, 0)
    step(qi, True)
    o_ref[...] = (acc_sc[...] / l_sc[...] * _silu(g_ref[...])).astype(o_ref.dtype)


def _flash(q, k, v, z, tq):
    h, s, _ = q.shape
    scale = (QK_NOPE + QK_ROPE) ** -0.5
    gcol = Z_GMLA // V_DIM
    return pl.pallas_call(
        functools.partial(_flash_kernel, tq=tq, scale=scale),
        grid=(h, s // tq),
        in_specs=[pl.BlockSpec((1, tq, HEAD_PAD), lambda hh, i: (hh, i, 0)),
                  pl.BlockSpec((1, s, HEAD_PAD), lambda hh, i: (hh, 0, 0)),
                  pl.BlockSpec((1, s, V_DIM), lambda hh, i: (hh, 0, 0)),
                  pl.BlockSpec((tq, V_DIM), lambda hh, i: (i, gcol + hh))],
        out_specs=pl.BlockSpec((tq, V_DIM), lambda hh, i: (i, hh)),
        out_shape=jax.ShapeDtypeStruct((s, h * V_DIM), BF),
        scratch_shapes=[pltpu.VMEM((tq, 1), F32), pltpu.VMEM((tq, 1), F32),
                        pltpu.VMEM((tq, V_DIM), F32)],
        compiler_params=_cp("parallel", "arbitrary"),
        name="flash",
    )(q, k, v, z)


def _decode_kernel(pt_ref, q_ref, cn_ref, kn_ref, *rest, npg, scale):
    del pt_ref
    c_refs, r_refs = rest[:npg], rest[npg:2 * npg]
    o_ref, cb, m_sc, l_sc, acc_sc = rest[2 * npg:]
    j = pl.program_id(1)

    @pl.when(j == 0)
    def _():
        m_sc[...] = jnp.full(m_sc.shape, -jnp.inf, F32)
        l_sc[...] = jnp.zeros(l_sc.shape, F32)
        acc_sc[...] = jnp.zeros(acc_sc.shape, F32)
        cb[:, KV_RANK:] = jnp.zeros((cb.shape[0], LANES), BF)

    for p in range(npg):
        rows = slice(p * PAGE_SIZE, (p + 1) * PAGE_SIZE)
        cb[rows, :KV_RANK] = c_refs[p][0].astype(BF)
        cb[rows, KV_RANK:KV_RANK + QK_ROPE] = r_refs[p][0].astype(BF)

    q = q_ref[0]
    s = _dot_t(q, cb[...]) * scale
    m_prev = m_sc[...]
    m_new = jnp.maximum(m_prev, jnp.max(s, axis=1, keepdims=True))
    alpha = jnp.exp(m_prev - m_new)
    p = jnp.exp(s - m_new)
    l_new = alpha * l_sc[...] + jnp.sum(p, axis=1, keepdims=True)
    acc_new = alpha * acc_sc[...] + _dot(p.astype(BF), cb[:, :KV_RANK])
    m_sc[...] = m_new
    l_sc[...] = l_new
    acc_sc[...] = acc_new

    @pl.when(j == pl.num_programs(1) - 1)
    def _():
        qf = q.astype(F32)
        cn = cn_ref[0].astype(F32)
        kn = kn_ref[0].astype(F32)
        s_new = (jnp.sum(qf[:, :KV_RANK] * cn, axis=1, keepdims=True)
                 + jnp.sum(qf[:, KV_RANK:] * kn, axis=1, keepdims=True)) * scale
        m_fin = jnp.maximum(m_new, s_new)
        a_fin = jnp.exp(m_new - m_fin)
        p_new = jnp.exp(s_new - m_fin)
        l_fin = a_fin * l_new + p_new
        out = (a_fin * acc_new + p_new.astype(BF).astype(F32) * cn) / l_fin
        for h in range(N_HEADS):
            o_ref[0, :, h * KV_RANK:(h + 1) * KV_RANK] = out[h:h + 1, :]


def _decode(page_table, qcat, ckvb, krb, cache_ckv, cache_krope, npg):
    b, n_pages = page_table.shape
    scale = (QK_NOPE + QK_ROPE) ** -0.5
    per_b = lambda bb, j, pt: (bb, 0, 0)
    page = lambda k: (lambda bb, j, pt: (pt[bb, j * npg + k], 0, 0))
    in_specs = [pl.BlockSpec((1, N_HEADS, LAT_PAD), per_b),
                pl.BlockSpec((1, 1, KV_RANK), per_b),
                pl.BlockSpec((1, 1, LANES), per_b)]
    in_specs += [pl.BlockSpec((1, PAGE_SIZE, KV_RANK), page(k)) for k in range(npg)]
    in_specs += [pl.BlockSpec((1, PAGE_SIZE, QK_ROPE), page(k)) for k in range(npg)]
    grid_spec = pltpu.PrefetchScalarGridSpec(
        num_scalar_prefetch=1,
        grid=(b, n_pages // npg),
        in_specs=in_specs,
        out_specs=pl.BlockSpec((1, 1, N_HEADS * KV_RANK), per_b),
        scratch_shapes=[pltpu.VMEM((npg * PAGE_SIZE, LAT_PAD), BF),
                        pltpu.VMEM((N_HEADS, 1), F32), pltpu.VMEM((N_HEADS, 1), F32),
                        pltpu.VMEM((N_HEADS, KV_RANK), F32)])
    out = pl.pallas_call(
        functools.partial(_decode_kernel, npg=npg, scale=scale),
        grid_spec=grid_spec,
        out_shape=jax.ShapeDtypeStruct((b, 1, N_HEADS * KV_RANK), F32),
        compiler_params=_cp("parallel", "arbitrary"),
        name="decode",
    )(page_table, qcat, ckvb.reshape(b, 1, KV_RANK), krb.reshape(b, 1, LANES),
      *([cache_ckv] * npg), *([cache_krope] * npg))
    return out.reshape(b, N_HEADS * KV_RANK)


def _qlat_kernel(q_ref, w_ref, o_ref):
    q = q_ref[0]
    o_ref[0, :, :KV_RANK] = _dot(q[:, :QK_NOPE], w_ref[0]).astype(BF)
    o_ref[0, :, KV_RANK:] = q[:, QK_NOPE:]


def _qlat(qs, wuk_t):
    h, b, _ = qs.shape
    return pl.pallas_call(
        _qlat_kernel,
        grid=(h,),
        in_specs=[pl.BlockSpec((1, b, HEAD_PAD), lambda i: (i, 0, 0)),
                  pl.BlockSpec((1, QK_NOPE, KV_RANK), lambda i: (i, 0, 0))],
        out_specs=pl.BlockSpec((1, b, LAT_PAD), lambda i: (i, 0, 0)),
        out_shape=jax.ShapeDtypeStruct((h, b, LAT_PAD), BF),
        compiler_params=_cp("parallel"),
        name="qlat",
    )(qs, wuk_t)


def _uvproj_kernel(l_ref, w_ref, g_ref, o_ref):
    o = _dot(l_ref[...].astype(BF), w_ref[0])
    o_ref[...] = (o * _silu(g_ref[...])).astype(o_ref.dtype)


def _uvproj(lat, wuv_t, z):
    b = lat.shape[0]
    gcol = Z_GMLA // V_DIM
    return pl.pallas_call(
        _uvproj_kernel,
        grid=(N_HEADS,),
        in_specs=[pl.BlockSpec((b, KV_RANK), lambda h: (0, h)),
                  pl.BlockSpec((1, KV_RANK, V_DIM), lambda h: (h, 0, 0)),
                  pl.BlockSpec((b, V_DIM), lambda h: (0, gcol + h))],
        out_specs=pl.BlockSpec((b, V_DIM), lambda h: (0, h)),
        out_shape=jax.ShapeDtypeStruct((b, N_HEADS * V_DIM), BF),
        compiler_params=_cp("parallel"),
        name="uvproj",
    )(lat, wuv_t, z)


def _rg_gates(xc, wa_ref, wi_ref, ba_ref, bi_ref, lam_ref):
    xcb = xc.astype(BF)
    blk = lambda n: slice(n * RG_BLOCK, (n + 1) * RG_BLOCK)
    ra = jnp.concatenate([_dot(xcb[:, blk(n)], wa_ref[n]) for n in range(RG_BLOCKS)], axis=1)
    ri = jnp.concatenate([_dot(xcb[:, blk(n)], wi_ref[n]) for n in range(RG_BLOCKS)], axis=1)
    r = jax.nn.sigmoid(ra + ba_ref[...])
    i = jax.nn.sigmoid(ri + bi_ref[...])
    log_a = -RG_C * r * jax.nn.softplus(-lam_ref[...])
    a = jnp.exp(log_a)
    u = jnp.sqrt(-_expm1(2.0 * log_a)) * (i * xc)
    return a, u


def _expm1(x):
    e = jnp.exp(x)
    em = e - 1.0
    ok = jnp.logical_and(em != 0.0, e > 0.0)
    kahan = em * x / jnp.log(jnp.where(ok, e, 2.0))
    return jnp.where(ok, kahan, jnp.where(em == 0.0, x, em))


def _rg_kernel(xr_ref, g_ref, cw_ref, cb_ref, wa_ref, wi_ref, ba_ref, bi_ref, lam_ref,
               y_ref, conv_ref, ht_ref, xp_sc, h_sc, a_sc, u_sc, *, tt):
    t = pl.program_id(0)
    pad = 8
    d = xr_ref.shape[1]

    @pl.when(t == 0)
    def _():
        xp_sc[0:pad, :] = jnp.zeros((pad, d), F32)
        h_sc[...] = jnp.zeros(h_sc.shape, F32)

    @pl.when(t > 0)
    def _():
        xp_sc[0:pad, :] = xp_sc[tt:tt + pad, :]

    x = xr_ref[...]
    xp_sc[pad:pad + tt, :] = x
    cw = cw_ref[...]
    xc = cb_ref[...] + cw[0:1] * xp_sc[pad - 3:pad - 3 + tt, :]
    xc = xc + cw[1:2] * xp_sc[pad - 2:pad - 2 + tt, :]
    xc = xc + cw[2:3] * xp_sc[pad - 1:pad - 1 + tt, :]
    xc = xc + cw[3:4] * x
    a, u = _rg_gates(xc, wa_ref, wi_ref, ba_ref, bi_ref, lam_ref)
    a_sc[...] = a
    u_sc[...] = u

    row = lax.broadcasted_iota(jnp.int32, (8, d), 0)

    def group(gi, h):
        s0 = pl.multiple_of(gi * 8, 8)
        aa = a_sc[pl.ds(s0, 8), :]
        uu = u_sc[pl.ds(s0, 8), :]
        for sh in (1, 2, 4):
            keep = row >= sh
            uu = jnp.where(keep, uu + aa * pltpu.roll(uu, sh, 0), uu)
            aa = jnp.where(keep, aa * pltpu.roll(aa, sh, 0), aa)
        hs = aa * h + uu
        u_sc[pl.ds(s0, 8), :] = hs
        return jnp.broadcast_to(hs[7:8, :], (8, d))

    h = lax.fori_loop(0, tt // 8, group, h_sc[...])
    h_sc[...] = h
    y_ref[...] = (u_sc[...] * _silu(g_ref[...])).astype(y_ref.dtype)

    @pl.when(t == pl.num_programs(0) - 1)
    def _():
        conv_ref[...] = xp_sc[pad + tt - (CONV_W - 1):pad + tt, :]
        ht_ref[...] = h[0:1, :]


def _rg_prompt(z, conv_w, conv_b, wa, wi, ba, bi, lam, tt):
    s = z.shape[0]
    d = conv_w.shape[1]
    fix2 = lambda i: (0, 0)
    fix3 = lambda i: (0, 0, 0)
    return pl.pallas_call(
        functools.partial(_rg_kernel, tt=tt),
        grid=(s // tt,),
        in_specs=[pl.BlockSpec((tt, d), lambda i: (i, Z_XR // d)),
                  pl.BlockSpec((tt, d), lambda i: (i, Z_GRG // d)),
                  pl.BlockSpec((CONV_W, d), fix2), pl.BlockSpec((1, d), fix2),
                  pl.BlockSpec(wa.shape, fix3), pl.BlockSpec(wi.shape, fix3),
                  pl.BlockSpec((1, d), fix2), pl.BlockSpec((1, d), fix2), pl.BlockSpec((1, d), fix2)],
        out_specs=[pl.BlockSpec((tt, d), lambda i: (i, 0)),
                   pl.BlockSpec((CONV_W - 1, d), fix2), pl.BlockSpec((1, d), fix2)],
        out_shape=[jax.ShapeDtypeStruct((s, d), BF),
                   jax.ShapeDtypeStruct((CONV_W - 1, d), F32), jax.ShapeDtypeStruct((1, d), F32)],
        scratch_shapes=[pltpu.VMEM((tt + 8, d), F32), pltpu.VMEM((8, d), F32),
                        pltpu.VMEM((tt, d), F32), pltpu.VMEM((tt, d), F32)],
        compiler_params=_cp("arbitrary"),
        name="rg_prompt",
    )(z, z, conv_w, conv_b.reshape(1, d), wa, wi, ba.reshape(1, d), bi.reshape(1, d), lam.reshape(1, d))


def _rg_sample_kernel(xr_ref, g_ref, sc_ref, h0_ref, cw_ref, cb_ref, wa_ref, wi_ref, ba_ref, bi_ref,
                      lam_ref, y_ref, conv_ref, h_ref):
    d = xr_ref.shape[1]
    x = xr_ref[...]
    cw = cw_ref[...]
    xc = cb_ref[...] + cw[0:1] * sc_ref[:, 0:d]
    xc = xc + cw[1:2] * sc_ref[:, d:2 * d]
    xc = xc + cw[2:3] * sc_ref[:, 2 * d:3 * d]
    xc = xc + cw[3:4] * x
    a, u = _rg_gates(xc, wa_ref, wi_ref, ba_ref, bi_ref, lam_ref)
    h = a * h0_ref[...] + u
    h_ref[...] = h
    y_ref[...] = (h * _silu(g_ref[...])).astype(y_ref.dtype)
    conv_ref[:, 0:d] = sc_ref[:, d:2 * d]
    conv_ref[:, d:2 * d] = sc_ref[:, 2 * d:3 * d]
    conv_ref[:, 2 * d:3 * d] = x


def _rg_sample(z, state_conv, h0, conv_w, conv_b, wa, wi, ba, bi, lam):
    b = z.shape[0]
    d = conv_w.shape[1]
    nprev = CONV_W - 1
    fix2 = lambda i: (0, 0)
    fix3 = lambda i: (0, 0, 0)
    return pl.pallas_call(
        _rg_sample_kernel,
        grid=(1,),
        in_specs=[pl.BlockSpec((b, d), lambda i: (0, Z_XR // d)),
                  pl.BlockSpec((b, d), lambda i: (0, Z_GRG // d)),
                  pl.BlockSpec((b, nprev * d), fix2), pl.BlockSpec((b, d), fix2),
                  pl.BlockSpec((CONV_W, d), fix2), pl.BlockSpec((1, d), fix2),
                  pl.BlockSpec(wa.shape, fix3), pl.BlockSpec(wi.shape, fix3),
                  pl.BlockSpec((1, d), fix2), pl.BlockSpec((1, d), fix2), pl.BlockSpec((1, d), fix2)],
        out_specs=[pl.BlockSpec((b, d), fix2), pl.BlockSpec((b, nprev * d), fix2),
                   pl.BlockSpec((b, d), fix2)],
        out_shape=[jax.ShapeDtypeStruct((b, d), BF), jax.ShapeDtypeStruct((b, nprev * d), F32),
                   jax.ShapeDtypeStruct((b, d), F32)],
        compiler_params=_cp("arbitrary"),
        name="rg_sample",
    )(z, z, state_conv.reshape(b, nprev * d), h0, conv_w, conv_b.reshape(1, d), wa, wi,
      ba.reshape(1, d), bi.reshape(1, d), lam.reshape(1, d))


def _memattn_kernel(q_ref, g_ref, k_ref, v_ref, o_ref):
    scale = MEM_DIM ** -0.5
    for h in range(MEM_HEADS):
        hs = slice(h * MEM_DIM, (h + 1) * MEM_DIM)
        s = _dot_t(q_ref[:, hs].astype(BF), k_ref[:, hs].astype(BF)) * scale
        p = jnp.exp(s - jnp.max(s, axis=1, keepdims=True))
        o = _dot(p.astype(BF), v_ref[:, hs].astype(BF)) / jnp.sum(p, axis=1, keepdims=True)
        o_ref[:, hs] = (o * _silu(g_ref[:, hs])).astype(o_ref.dtype)


def _memattn(z, k, v, tm):
    s = z.shape[0]
    d = MEM_HEADS * MEM_DIM
    nm = k.shape[0]
    return pl.pallas_call(
        _memattn_kernel,
        grid=(s // tm,),
        in_specs=[pl.BlockSpec((tm, d), lambda i: (i, Z_QMEM // d)),
                  pl.BlockSpec((tm, d), lambda i: (i, Z_GMEM // d)),
                  pl.BlockSpec((nm, d), lambda i: (0, 0)), pl.BlockSpec((nm, d), lambda i: (0, 0))],
        out_specs=pl.BlockSpec((tm, d), lambda i: (i, 0)),
        out_shape=jax.ShapeDtypeStruct((s, d), BF),
        compiler_params=_cp("parallel"),
        name="memattn",
    )(z, z, k, v)


def _memattn_s_kernel(q_ref, g_ref, k_ref, v_ref, o_ref, *, bt):
    scale = MEM_DIM ** -0.5
    d = MEM_HEADS * MEM_DIM
    lane = lax.broadcasted_iota(jnp.int32, (8, d), 1)
    rowi = lax.broadcasted_iota(jnp.int32, (8, d), 0)
    own = (lane // MEM_DIM) == rowi
    for b in range(bt):
        qb = jnp.broadcast_to(q_ref[b:b + 1, :], (8, d))
        qm = jnp.where(own, qb, 0.0).astype(BF)
        s = _dot_t(qm, k_ref[b].astype(BF)) * scale
        p = jnp.exp(s - jnp.max(s, axis=1, keepdims=True))
        o = _dot(p.astype(BF), v_ref[b].astype(BF)) / jnp.sum(p, axis=1, keepdims=True)
        ob = jnp.sum(jnp.where(own, o, 0.0), axis=0, keepdims=True)
        o_ref[b:b + 1, :] = ob * _silu(g_ref[b:b + 1, :])


def _memattn_s(z, k, v, bt):
    b, nm, d = k.shape
    return pl.pallas_call(
        functools.partial(_memattn_s_kernel, bt=bt),
        grid=(b // bt,),
        in_specs=[pl.BlockSpec((bt, d), lambda i: (i, Z_QMEM // d)),
                  pl.BlockSpec((bt, d), lambda i: (i, Z_GMEM // d)),
                  pl.BlockSpec((bt, nm, d), lambda i: (i, 0, 0)),
                  pl.BlockSpec((bt, nm, d), lambda i: (i, 0, 0))],
        out_specs=pl.BlockSpec((bt, d), lambda i: (i, 0)),
        out_shape=jax.ShapeDtypeStruct((b, d), F32),
        compiler_params=_cp("parallel"),
        name="memattn_s",
    )(z, z, k, v)


def _merge1_kernel(a_ref, b_ref, c_ref, wa_ref, wb_ref, wc_ref, ma_ref, mb_ref, mc_ref, o_ref):
    sg = jax.nn.sigmoid
    o = sg(ma_ref[...]) * _dot(a_ref[...].astype(BF), wa_ref[...])
    o = o + sg(mb_ref[...]) * _dot(b_ref[...].astype(BF), wb_ref[...])
    o = o + sg(mc_ref[...]) * _dot(c_ref[...].astype(BF), wc_ref[...])
    o_ref[...] = o.astype(o_ref.dtype)


def _merge1(a, b, c, wa, wb, wc, z, tm, tn):
    m = a.shape[0]
    n = wa.shape[1]
    row = lambda i, j: (i, 0)
    col = lambda i, j: (0, j)
    zcol = lambda off: (lambda i, j: (i, off // tn + j))
    return pl.pallas_call(
        _merge1_kernel,
        grid=(m // tm, n // tn),
        in_specs=[pl.BlockSpec((tm, a.shape[1]), row), pl.BlockSpec((tm, b.shape[1]), row),
                  pl.BlockSpec((tm, c.shape[1]), row),
                  pl.BlockSpec((wa.shape[0], tn), col), pl.BlockSpec((wb.shape[0], tn), col),
                  pl.BlockSpec((wc.shape[0], tn), col),
                  pl.BlockSpec((tm, tn), zcol(Z_MRG)), pl.BlockSpec((tm, tn), zcol(Z_MMLA)),
                  pl.BlockSpec((tm, tn), zcol(Z_MMEM))],
        out_specs=pl.BlockSpec((tm, tn), lambda i, j: (i, j)),
        out_shape=jax.ShapeDtypeStruct((m, n), BF),
        compiler_params=_cp("parallel", "parallel"),
        name="merge1",
    )(a, b, c, wa, wb, wc, z, z, z)


def _merge2_kernel(z_ref, w_ref, x_ref, g_ref, o_ref):
    o_ref[...] = _rms(x_ref[...] + _dot(z_ref[...], w_ref[...]), g_ref[...])


def _merge2(zz, w, x, g, tm):
    m, d = x.shape
    return pl.pallas_call(
        _merge2_kernel,
        grid=(m // tm,),
        in_specs=[pl.BlockSpec((tm, d), lambda i: (i, 0)), pl.BlockSpec(w.shape, lambda i: (0, 0)),
                  pl.BlockSpec((tm, d), lambda i: (i, 0)), pl.BlockSpec((1, d), lambda i: (0, 0))],
        out_specs=pl.BlockSpec((tm, d), lambda i: (i, 0)),
        out_shape=jax.ShapeDtypeStruct((m, d), F32),
        compiler_params=_cp("parallel"),
        name="merge2",
    )(zz, w, x, g.reshape(1, d))


def _rope_tables(pos):
    inv = ROPE_THETA ** (-jnp.arange(0, QK_ROPE, 2, dtype=F32) / QK_ROPE)
    ang = pos.astype(F32)[:, None] * inv[None, :]
    cos, sin = jnp.cos(ang), jnp.sin(ang)
    zero = jnp.zeros((pos.shape[0], LANES - QK_ROPE), F32)
    return (jnp.concatenate([cos, cos, zero], axis=1),
            jnp.concatenate([-sin, sin, zero], axis=1))


def kernel(x_prompt, x_sample, mem_prompt, cache_ckv, cache_krope, cache_mem_k, cache_mem_v, state_conv, state_rglru, page_table, norm_in, w_in, conv_w, conv_b, rg_wa, rg_ba, rg_wi, rg_bi, rg_lambda, kv_norm, w_uk, w_uv, mem_norm, w_mk, w_mv, w_rg_o, w_mla_o, w_mem_o, w_out, final_norm):
    bp, seq, d_model = x_prompt.shape
    bd, dec_seq, _ = x_sample.shape
    assert bp == 1 and dec_seq == 1
    d_rnn = conv_w.shape[1]
    n_pages = page_table.shape[1]
    past_len = n_pages * PAGE_SIZE
    half = QK_ROPE // 2

    d_q = N_HEADS * (QK_NOPE + QK_ROPE)
    d_v = N_HEADS * V_DIM
    d_m = MEM_HEADS * MEM_DIM
    o = [0]
    for sz in (d_rnn, d_rnn, d_q, KV_RANK, QK_ROPE, d_v, d_m, d_m, d_model, d_model, d_model):
        o.append(o[-1] + sz)
    seg = lambda i: w_in[:, o[i]:o[i + 1]]
    w_main = jnp.concatenate([seg(0), seg(1), seg(5), seg(6), seg(7), seg(8), seg(9), seg(10)],
                             axis=1).astype(BF)
    wq3 = seg(2).reshape(d_model, N_HEADS, QK_NOPE + QK_ROPE)
    wq = jnp.concatenate([wq3[:, :, :QK_NOPE + QK_ROPE], wq3[:, :, QK_NOPE + half:],
                          wq3[:, :, QK_NOPE:QK_NOPE + half]], axis=2)
    wq = jnp.transpose(wq, (1, 0, 2)).astype(BF)
    wkr = seg(4)
    wkv = jnp.concatenate([seg(3), wkr, wkr[:, half:], wkr[:, :half]], axis=1).astype(BF)
    wuk2 = w_uk.reshape(KV_RANK, N_HEADS * QK_NOPE).astype(BF)
    wuv2 = w_uv.reshape(KV_RANK, d_v).astype(BF)
    wuk_t = jnp.transpose(w_uk, (1, 2, 0)).astype(BF)
    wuv_t = jnp.transpose(w_uv, (1, 0, 2)).astype(BF)
    w_rg_o_b, w_mla_o_b, w_mem_o_b, w_out_b = (w.astype(BF) for w in (w_rg_o, w_mla_o, w_mem_o, w_out))
    wa_b, wi_b = rg_wa.astype(BF), rg_wi.astype(BF)
    w_mkv = jnp.concatenate([w_mk, w_mv], axis=1).astype(BF)

    xp = x_prompt.reshape(seq, d_model)
    xs = x_sample.reshape(bd, d_model)

    cos_p, sin_p = _rope_tables(jnp.arange(seq))
    xn_p = _norm_cast(xp, norm_in, 512)
    z_p = _matmul(xn_p, w_main, 1024, 1024, F32, "inproj")
    q_p = _qproj(xn_p, wq, cos_p, sin_p, 1024)
    ckv_p, kr_p, ckvb_p, krb_p = _kvproj(xn_p, wkv, cos_p, sin_p, kv_norm, 512)
    k_p, v_p = _kvup(ckvb_p, krb_p, wuk2, wuv2, 512)
    b_p = _flash(q_p, k_p, v_p, z_p, 512)
    a_p, conv_p, h_p = _rg_prompt(z_p, conv_w, conv_b, wa_b, wi_b, rg_ba, rg_bi, rg_lambda, 512)
    mn = _norm_cast(mem_prompt.reshape(-1, d_model), mem_norm, 256)
    mkv = _matmul(mn, w_mkv, 256, 1024, F32, "memkv")
    mem_k, mem_v = mkv[:, :d_m], mkv[:, d_m:]
    c_p = _memattn(z_p, mem_k, mem_v, 512)
    zz_p = _merge1(a_p, b_p, c_p, w_rg_o_b, w_mla_o_b, w_mem_o_b, z_p, 512, 512)
    y_p = _merge2(zz_p, w_out_b, xp, final_norm, 512)

    cos_s, sin_s = _rope_tables(jnp.full((bd,), past_len))
    xn_s = _norm_cast(xs, norm_in, bd)
    z_s = _matmul(xn_s, w_main, bd, 1024, F32, "inproj_s")
    q_s = _qproj(xn_s, wq, cos_s, sin_s, bd)
    ckv_s, kr_s, ckvb_s, krb_s = _kvproj(xn_s, wkv, cos_s, sin_s, kv_norm, bd)
    qcat = jnp.transpose(_qlat(q_s, wuk_t), (1, 0, 2))
    lat = _decode(page_table, qcat, ckvb_s, krb_s, cache_ckv, cache_krope, 16)
    b_s = _uvproj(lat, wuv_t, z_s)
    a_s, conv_s, h_s = _rg_sample(z_s, state_conv, state_rglru, conv_w, conv_b, wa_b, wi_b,
                                  rg_ba, rg_bi, rg_lambda)
    c_s = _memattn_s(z_s, cache_mem_k.reshape(bd, -1, d_m), cache_mem_v.reshape(bd, -1, d_m), 8)
    zz_s = _merge1(a_s, b_s, c_s, w_rg_o_b, w_mla_o_b, w_mem_o_b, z_s, bd, 512)
    y_s = _merge2(zz_s, w_out_b, xs, final_norm, bd)

    n_mem = mem_prompt.shape[1]
    return (y_p.reshape(1, seq, d_model), y_s.reshape(bd, 1, d_model),
            ckv_p.reshape(1, seq, KV_RANK), kr_p.reshape(1, seq, QK_ROPE),
            conv_p.reshape(1, CONV_W - 1, d_rnn), h_p.reshape(1, d_rnn),
            mem_k.reshape(1, n_mem, MEM_HEADS, MEM_DIM), mem_v.reshape(1, n_mem, MEM_HEADS, MEM_DIM),
            ckv_s.reshape(bd, 1, KV_RANK), kr_s.reshape(bd, 1, QK_ROPE),
            conv_s.reshape(bd, CONV_W - 1, d_rnn), h_s)
```

```python
import functools
import math

import jax
import jax.numpy as jnp
from jax import lax
from jax.experimental import pallas as pl
from jax.experimental.pallas import tpu as pltpu

F32 = jnp.float32
BF = jnp.bfloat16

EPS = 1e-6
RG_BLOCKS = 8
RG_BLOCK = 128
CONV_W = 4
RG_C = 8.0
N_HEADS = 16
QK_NOPE = 128
QK_ROPE = 64
V_DIM = 128
KV_RANK = 512
ROPE_THETA = 10000.0
MEM_HEADS = 4
MEM_DIM = 256
PAGE_SIZE = 128
LANES = 128
HEAD_PAD = 256
LAT_PAD = KV_RANK + LANES
VMEM_LIMIT = 48 * 1024 * 1024
QK_SCALE2 = (QK_NOPE + QK_ROPE) ** -0.5 * math.log2(math.e)

Z_XR, Z_GRG, Z_GMLA, Z_QMEM, Z_GMEM, Z_MRG, Z_MMLA, Z_MMEM, Z_COLS = (
    0, 1024, 2048, 4096, 5120, 6144, 8192, 10240, 12288)


def _cp(*sem):
    return pltpu.CompilerParams(dimension_semantics=sem, vmem_limit_bytes=VMEM_LIMIT)


def _silu(g):
    return g * jax.nn.sigmoid(g)


def _rms(x, g):
    return x * lax.rsqrt(jnp.mean(x * x, axis=-1, keepdims=True) + EPS) * g


def _dot(a, b):
    return jnp.dot(a, b, preferred_element_type=F32)


def _dot_t(a, b):
    return lax.dot_general(a, b, (((1,), (1,)), ((), ())), preferred_element_type=F32)


def _norm_cast_kernel(x_ref, g_ref, o_ref):
    o_ref[...] = _rms(x_ref[...], g_ref[...]).astype(o_ref.dtype)


def _norm_cast(x, g, tm):
    m, d = x.shape
    return pl.pallas_call(
        _norm_cast_kernel,
        grid=(m // tm,),
        in_specs=[pl.BlockSpec((tm, d), lambda i: (i, 0)),
                  pl.BlockSpec((1, d), lambda i: (0, 0))],
        out_specs=pl.BlockSpec((tm, d), lambda i: (i, 0)),
        out_shape=jax.ShapeDtypeStruct((m, d), BF),
        compiler_params=_cp("parallel"),
        name="norm_cast",
    )(x, g.reshape(1, d))


def _mm_kernel(a_ref, w_ref, o_ref):
    o_ref[...] = _dot(a_ref[...], w_ref[...]).astype(o_ref.dtype)


def _matmul(a, w, tm, tn, out_dtype, name):
    m, k = a.shape
    n = w.shape[1]
    return pl.pallas_call(
        _mm_kernel,
        grid=(m // tm, n // tn),
        in_specs=[pl.BlockSpec((tm, k), lambda i, j: (i, 0)),
                  pl.BlockSpec((k, tn), lambda i, j: (0, j))],
        out_specs=pl.BlockSpec((tm, tn), lambda i, j: (i, j)),
        out_shape=jax.ShapeDtypeStruct((m, n), out_dtype),
        compiler_params=_cp("parallel", "parallel"),
        name=name,
    )(a, w)


def _rope_hi(hi, c, s):
    return hi * c + pltpu.roll(hi, QK_ROPE, 1) * s


def _qproj_kernel(a_ref, w_ref, c_ref, s_ref, o_ref):
    res = _dot(a_ref[...], w_ref[0]) * QK_SCALE2
    o_ref[0, :, :QK_NOPE] = res[:, :QK_NOPE].astype(BF)
    o_ref[0, :, QK_NOPE:] = _rope_hi(res[:, QK_NOPE:], c_ref[...], s_ref[...]).astype(BF)


def _qproj(xn, wq, cos_t, sin_t, tm):
    m, d = xn.shape
    return pl.pallas_call(
        _qproj_kernel,
        grid=(m // tm, N_HEADS),
        in_specs=[pl.BlockSpec((tm, d), lambda i, h: (i, 0)),
                  pl.BlockSpec((1, d, HEAD_PAD), lambda i, h: (h, 0, 0)),
                  pl.BlockSpec((tm, LANES), lambda i, h: (i, 0)),
                  pl.BlockSpec((tm, LANES), lambda i, h: (i, 0))],
        out_specs=pl.BlockSpec((1, tm, HEAD_PAD), lambda i, h: (h, i, 0)),
        out_shape=jax.ShapeDtypeStruct((N_HEADS, m, HEAD_PAD), BF),
        compiler_params=_cp("parallel", "parallel"),
        name="qproj",
    )(xn, wq, cos_t, sin_t)


def _kvproj_kernel(a_ref, w_ref, c_ref, s_ref, g_ref, ckv_ref, kr_ref, ckvb_ref, krb_ref):
    res = _dot(a_ref[...], w_ref[...])
    ckv = _rms(res[:, :KV_RANK], g_ref[...])
    ckv_ref[...] = ckv
    ckvb_ref[...] = ckv.astype(BF)
    rot = _rope_hi(res[:, KV_RANK:], c_ref[...], s_ref[...])
    kr_ref[...] = rot[:, :QK_ROPE]
    krb_ref[...] = rot.astype(BF)


def _kvproj(xn, wkv, cos_t, sin_t, kv_norm, tm):
    m, d = xn.shape
    n = wkv.shape[1]
    row = lambda i: (i, 0)
    fix = lambda i: (0, 0)
    return pl.pallas_call(
        _kvproj_kernel,
        grid=(m // tm,),
        in_specs=[pl.BlockSpec((tm, d), row), pl.BlockSpec((d, n), fix),
                  pl.BlockSpec((tm, LANES), row), pl.BlockSpec((tm, LANES), row),
                  pl.BlockSpec((1, KV_RANK), fix)],
        out_specs=[pl.BlockSpec((tm, KV_RANK), row), pl.BlockSpec((tm, QK_ROPE), row),
                   pl.BlockSpec((tm, KV_RANK), row), pl.BlockSpec((tm, LANES), row)],
        out_shape=[jax.ShapeDtypeStruct((m, KV_RANK), F32), jax.ShapeDtypeStruct((m, QK_ROPE), F32),
                   jax.ShapeDtypeStruct((m, KV_RANK), BF), jax.ShapeDtypeStruct((m, LANES), BF)],
        compiler_params=_cp("parallel"),
        name="kvproj",
    )(xn, wkv, cos_t, sin_t, kv_norm.reshape(1, KV_RANK))


def _kvup_kernel(c_ref, kr_ref, wk_ref, wv_ref, k_ref, v_ref):
    c = c_ref[...]
    kn = _dot(c, wk_ref[...])
    v = _dot(c, wv_ref[...])
    kr = kr_ref[...]
    ones = jnp.ones((c.shape[0], V_DIM), BF)
    for h in range(N_HEADS):
        k_ref[h, :, :QK_NOPE] = kn[:, h * QK_NOPE:(h + 1) * QK_NOPE].astype(BF)
        k_ref[h, :, QK_NOPE:] = kr
        v_ref[h, :, :V_DIM] = v[:, h * V_DIM:(h + 1) * V_DIM].astype(BF)
        v_ref[h, :, V_DIM:] = ones


def _kvup(ckvb, krb, wuk, wuv, tm):
    s = ckvb.shape[0]
    row = lambda i: (i, 0)
    fix = lambda i: (0, 0)
    return pl.pallas_call(
        _kvup_kernel,
        grid=(s // tm,),
        in_specs=[pl.BlockSpec((tm, KV_RANK), row), pl.BlockSpec((tm, LANES), row),
                  pl.BlockSpec(wuk.shape, fix), pl.BlockSpec(wuv.shape, fix)],
        out_specs=[pl.BlockSpec((N_HEADS, tm, HEAD_PAD), lambda i: (0, i, 0)),
                   pl.BlockSpec((N_HEADS, tm, 2 * V_DIM), lambda i: (0, i, 0))],
        out_shape=[jax.ShapeDtypeStruct((N_HEADS, s, HEAD_PAD), BF),
                   jax.ShapeDtypeStruct((N_HEADS, s, 2 * V_DIM), BF)],
        compiler_params=_cp("parallel"),
        name="kvup",
    )(ckvb, krb, wuk, wuv)


def _flash_kernel(q_ref, k_ref, v_ref, g_ref, o_ref, sa, sb, mxa, mxb, m_sc, acc_sc, *, tq):
    qi = pl.program_id(1)
    q = q_ref[0]
    m_sc[...] = jnp.full(m_sc.shape, -jnp.inf, F32)
    acc_sc[...] = jnp.zeros(acc_sc.shape, F32)

    def stage_x(t, s_buf, mx_buf):
        start = pl.multiple_of(t * tq, tq)
        s = _dot_t(q, k_ref[0, pl.ds(start, tq), :])
        s_buf[...] = s
        mx_buf[...] = jnp.broadcast_to(jnp.max(s, axis=1, keepdims=True), mx_buf.shape)

    def stage_y(t, s_buf, mx_buf, masked):
        s = s_buf[...]
        if masked:
            row = lax.broadcasted_iota(jnp.int32, (tq, tq), 0)
            col = lax.broadcasted_iota(jnp.int32, (tq, tq), 1)
            s = jnp.where(col <= row, s, -jnp.inf)
            mx = jnp.max(s, axis=1, keepdims=True)
        else:
            mx = mx_buf[...]
        m_prev = m_sc[...]
        m_new = jnp.maximum(m_prev, mx)
        m_sc[...] = m_new
        p = jnp.exp2(s - jnp.tile(m_new, (1, tq // LANES))).astype(BF)
        alpha = jnp.exp2(m_prev - m_new)
        start = pl.multiple_of(t * tq, tq)
        acc_sc[...] = jnp.tile(alpha, (1, 2)) * acc_sc[...] + _dot(p, v_ref[0, pl.ds(start, tq), :])

    stage_x(0, sa, mxa)

    def pair(i, carry):
        t = 2 * i
        stage_x(t + 1, sb, mxb)
        stage_y(t, sa, mxa, False)
        stage_x(t + 2, sa, mxa)
        stage_y(t + 1, sb, mxb, False)
        return carry

    lax.fori_loop(0, qi // 2, pair, 0)

    @pl.when(qi % 2 == 1)
    def _():
        stage_x(qi, sb, mxb)
        stage_y(qi - 1, sa, mxa, False)
        stage_y(qi, sb, mxb, True)

    @pl.when(qi % 2 == 0)
    def _():
        stage_y(qi, sa, mxa, True)

    o = acc_sc[:, :V_DIM] / acc_sc[:, V_DIM:]
    o_ref[...] = (o * _silu(g_ref[...])).astype(o_ref.dtype)


def _flash(q, k, v, z, tq):
    h, s, _ = q.shape
    gcol = Z_GMLA // V_DIM
    return pl.pallas_call(
        functools.partial(_flash_kernel, tq=tq),
        grid=(h, s // tq),
        in_specs=[pl.BlockSpec((1, tq, HEAD_PAD), lambda hh, i: (hh, i, 0)),
                  pl.BlockSpec((1, s, HEAD_PAD), lambda hh, i: (hh, 0, 0)),
                  pl.BlockSpec((1, s, 2 * V_DIM), lambda hh, i: (hh, 0, 0)),
                  pl.BlockSpec((tq, V_DIM), lambda hh, i: (i, gcol + hh))],
        out_specs=pl.BlockSpec((tq, V_DIM), lambda hh, i: (i, hh)),
        out_shape=jax.ShapeDtypeStruct((s, h * V_DIM), BF),
        scratch_shapes=[pltpu.VMEM((tq, tq), F32), pltpu.VMEM((tq, tq), F32),
                        pltpu.VMEM((tq, LANES), F32), pltpu.VMEM((tq, LANES), F32),
                        pltpu.VMEM((tq, LANES), F32), pltpu.VMEM((tq, 2 * V_DIM), F32)],
        compiler_params=_cp("parallel", "arbitrary"),
        name="flash",
    )(q, k, v, z)


def _decode_kernel(pt_ref, q_ref, cn_ref, kn_ref, ckv_hbm, kr_hbm, o_ref, cbuf, rbuf, sem, *, npg, nchunk):
    b = pl.program_id(0)
    nb = pl.num_programs(0)

    def copies(seq, j, slot, wait_only=False):
        cps = []
        for p in range(npg):
            page = 0 if wait_only else pt_ref[(seq * nchunk + j) * npg + p]
            keys = pl.ds(p * PAGE_SIZE, PAGE_SIZE)
            cps.append(pltpu.make_async_copy(ckv_hbm.at[page], cbuf.at[slot, keys, :], sem.at[0, slot]))
            cps.append(pltpu.make_async_copy(kr_hbm.at[page], rbuf.at[slot, :, keys], sem.at[1, slot]))
        return cps

    @pl.when(b == 0)
    def _():
        for cp in copies(0, 0, 0):
            cp.start()

    q = q_ref[0]
    ql = q[:, :KV_RANK]
    qr = q[:, KV_RANK:KV_RANK + QK_ROPE]
    m = jnp.full((N_HEADS, 1), -jnp.inf, F32)
    l = jnp.zeros((N_HEADS, 1), F32)
    acc = jnp.zeros((N_HEADS, KV_RANK), F32)
    for j in range(nchunk):
        slot = j % 2
        if j + 1 < nchunk:
            for cp in copies(b, j + 1, 1 - slot):
                cp.start()
        else:
            @pl.when(b + 1 < nb)
            def _():
                for cp in copies(b + 1, 0, 1 - slot):
                    cp.start()
        for cp in copies(b, j, slot, wait_only=True):
            cp.wait()
        cb = cbuf[slot].astype(BF)
        s = _dot_t(ql, cb) + _dot(qr, rbuf[slot].astype(BF))
        m_new = jnp.maximum(m, jnp.max(s, axis=1, keepdims=True))
        alpha = jnp.exp2(m - m_new)
        p = jnp.exp2(s - m_new)
        l = alpha * l + jnp.sum(p, axis=1, keepdims=True)
        acc = alpha * acc + _dot(p.astype(BF), cb)
        m = m_new

    qf = q.astype(F32)
    cn = cn_ref[0].astype(F32)
    kn = kn_ref[0].astype(F32)
    s_new = (jnp.sum(qf[:, :KV_RANK] * cn, axis=1, keepdims=True)
             + jnp.sum(qf[:, KV_RANK:] * kn, axis=1, keepdims=True))
    m_fin = jnp.maximum(m, s_new)
    a_fin = jnp.exp2(m - m_fin)
    p_new = jnp.exp2(s_new - m_fin)
    l_fin = a_fin * l + p_new
    out = (a_fin * acc + p_new.astype(BF).astype(F32) * cn) / l_fin
    for h in range(N_HEADS):
        o_ref[0, :, h * KV_RANK:(h + 1) * KV_RANK] = out[h:h + 1, :]


def _decode(page_table, qcat, ckvb, krb, cache_ckv, cache_krope_t, npg):
    b, n_pages = page_table.shape
    nchunk = n_pages // npg
    assert nchunk * npg == n_pages and nchunk % 2 == 0
    per_b = lambda bb, pt: (bb, 0, 0)
    grid_spec = pltpu.PrefetchScalarGridSpec(
        num_scalar_prefetch=1,
        grid=(b,),
        in_specs=[pl.BlockSpec((1, N_HEADS, LAT_PAD), per_b),
                  pl.BlockSpec((1, 1, KV_RANK), per_b),
                  pl.BlockSpec((1, 1, LANES), per_b),
                  pl.BlockSpec(memory_space=pl.ANY),
                  pl.BlockSpec(memory_space=pl.ANY)],
        out_specs=pl.BlockSpec((1, 1, N_HEADS * KV_RANK), per_b),
        scratch_shapes=[pltpu.VMEM((2, npg * PAGE_SIZE, KV_RANK), F32),
                        pltpu.VMEM((2, QK_ROPE, npg * PAGE_SIZE), F32),
                        pltpu.SemaphoreType.DMA((2, 2))])
    out = pl.pallas_call(
        functools.partial(_decode_kernel, npg=npg, nchunk=nchunk),
        grid_spec=grid_spec,
        out_shape=jax.ShapeDtypeStruct((b, 1, N_HEADS * KV_RANK), F32),
        compiler_params=_cp("arbitrary"),
        name="decode",
    )(page_table.reshape(-1), qcat, ckvb.reshape(b, 1, KV_RANK), krb.reshape(b, 1, LANES),
      cache_ckv, cache_krope_t)
    return out.reshape(b, N_HEADS * KV_RANK)


def _qlat_kernel(q_ref, w_ref, o_ref):
    q = q_ref[0]
    o_ref[0, :, :KV_RANK] = _dot(q[:, :QK_NOPE], w_ref[0]).astype(BF)
    o_ref[0, :, KV_RANK:] = q[:, QK_NOPE:]


def _qlat(qs, wuk_t):
    h, b, _ = qs.shape
    return pl.pallas_call(
        _qlat_kernel,
        grid=(h,),
        in_specs=[pl.BlockSpec((1, b, HEAD_PAD), lambda i: (i, 0, 0)),
                  pl.BlockSpec((1, QK_NOPE, KV_RANK), lambda i: (i, 0, 0))],
        out_specs=pl.BlockSpec((1, b, LAT_PAD), lambda i: (i, 0, 0)),
        out_shape=jax.ShapeDtypeStruct((h, b, LAT_PAD), BF),
        compiler_params=_cp("parallel"),
        name="qlat",
    )(qs, wuk_t)


def _uvproj_kernel(l_ref, w_ref, g_ref, o_ref):
    o = _dot(l_ref[...].astype(BF), w_ref[0])
    o_ref[...] = (o * _silu(g_ref[...])).astype(o_ref.dtype)


def _uvproj(lat, wuv_t, z):
    b = lat.shape[0]
    gcol = Z_GMLA // V_DIM
    return pl.pallas_call(
        _uvproj_kernel,
        grid=(N_HEADS,),
        in_specs=[pl.BlockSpec((b, KV_RANK), lambda h: (0, h)),
                  pl.BlockSpec((1, KV_RANK, V_DIM), lambda h: (h, 0, 0)),
                  pl.BlockSpec((b, V_DIM), lambda h: (0, gcol + h))],
        out_specs=pl.BlockSpec((b, V_DIM), lambda h: (0, h)),
        out_shape=jax.ShapeDtypeStruct((b, N_HEADS * V_DIM), BF),
        compiler_params=_cp("parallel"),
        name="uvproj",
    )(lat, wuv_t, z)


def _rg_gates(xc, wa_ref, wi_ref, ba_ref, bi_ref, lam_ref):
    xcb = xc.astype(BF)
    blk = lambda n: slice(n * RG_BLOCK, (n + 1) * RG_BLOCK)
    ra = jnp.concatenate([_dot(xcb[:, blk(n)], wa_ref[n]) for n in range(RG_BLOCKS)], axis=1)
    ri = jnp.concatenate([_dot(xcb[:, blk(n)], wi_ref[n]) for n in range(RG_BLOCKS)], axis=1)
    r = jax.nn.sigmoid(ra + ba_ref[...])
    i = jax.nn.sigmoid(ri + bi_ref[...])
    log_a = -RG_C * r * jax.nn.softplus(-lam_ref[...])
    a = jnp.exp(log_a)
    u = jnp.sqrt(-_expm1(2.0 * log_a)) * (i * xc)
    return a, u


def _expm1(x):
    e = jnp.exp(x)
    em = e - 1.0
    ok = jnp.logical_and(em != 0.0, e > 0.0)
    kahan = em * x / jnp.log(jnp.where(ok, e, 2.0))
    return jnp.where(ok, kahan, jnp.where(em == 0.0, x, em))


def _rg_kernel(xr_ref, g_ref, cw_ref, cb_ref, wa_ref, wi_ref, ba_ref, bi_ref, lam_ref,
               y_ref, conv_ref, ht_ref, xp_sc, h_sc, a_sc, u_sc, *, tt):
    t = pl.program_id(0)
    pad = 8
    d = xr_ref.shape[1]

    @pl.when(t == 0)
    def _():
        xp_sc[0:pad, :] = jnp.zeros((pad, d), F32)
        h_sc[...] = jnp.zeros(h_sc.shape, F32)

    @pl.when(t > 0)
    def _():
        xp_sc[0:pad, :] = xp_sc[tt:tt + pad, :]

    x = xr_ref[...]
    xp_sc[pad:pad + tt, :] = x
    cw = cw_ref[...]
    xc = cb_ref[...] + cw[0:1] * xp_sc[pad - 3:pad - 3 + tt, :]
    xc = xc + cw[1:2] * xp_sc[pad - 2:pad - 2 + tt, :]
    xc = xc + cw[2:3] * xp_sc[pad - 1:pad - 1 + tt, :]
    xc = xc + cw[3:4] * x
    a, u = _rg_gates(xc, wa_ref, wi_ref, ba_ref, bi_ref, lam_ref)
    a_sc[...] = a
    u_sc[...] = u

    row = lax.broadcasted_iota(jnp.int32, (8, d), 0)

    def group(gi, h):
        s0 = pl.multiple_of(gi * 8, 8)
        aa = a_sc[pl.ds(s0, 8), :]
        uu = u_sc[pl.ds(s0, 8), :]
        for sh in (1, 2, 4):
            keep = row >= sh
            uu = jnp.where(keep, uu + aa * pltpu.roll(uu, sh, 0), uu)
            aa = jnp.where(keep, aa * pltpu.roll(aa, sh, 0), aa)
        hs = aa * h + uu
        u_sc[pl.ds(s0, 8), :] = hs
        return jnp.broadcast_to(hs[7:8, :], (8, d))

    h = lax.fori_loop(0, tt // 8, group, h_sc[...])
    h_sc[...] = h
    y_ref[...] = (u_sc[...] * _silu(g_ref[...])).astype(y_ref.dtype)

    @pl.when(t == pl.num_programs(0) - 1)
    def _():
        conv_ref[...] = xp_sc[pad + tt - (CONV_W - 1):pad + tt, :]
        ht_ref[...] = h[0:1, :]


def _rg_prompt(z, conv_w, conv_b, wa, wi, ba, bi, lam, tt):
    s = z.shape[0]
    d = conv_w.shape[1]
    fix2 = lambda i: (0, 0)
    fix3 = lambda i: (0, 0, 0)
    return pl.pallas_call(
        functools.partial(_rg_kernel, tt=tt),
        grid=(s // tt,),
        in_specs=[pl.BlockSpec((tt, d), lambda i: (i, Z_XR // d)),
                  pl.BlockSpec((tt, d), lambda i: (i, Z_GRG // d)),
                  pl.BlockSpec((CONV_W, d), fix2), pl.BlockSpec((1, d), fix2),
                  pl.BlockSpec(wa.shape, fix3), pl.BlockSpec(wi.shape, fix3),
                  pl.BlockSpec((1, d), fix2), pl.BlockSpec((1, d), fix2), pl.BlockSpec((1, d), fix2)],
        out_specs=[pl.BlockSpec((tt, d), lambda i: (i, 0)),
                   pl.BlockSpec((CONV_W - 1, d), fix2), pl.BlockSpec((1, d), fix2)],
        out_shape=[jax.ShapeDtypeStruct((s, d), BF),
                   jax.ShapeDtypeStruct((CONV_W - 1, d), F32), jax.ShapeDtypeStruct((1, d), F32)],
        scratch_shapes=[pltpu.VMEM((tt + 8, d), F32), pltpu.VMEM((8, d), F32),
                        pltpu.VMEM((tt, d), F32), pltpu.VMEM((tt, d), F32)],
        compiler_params=_cp("arbitrary"),
        name="rg_prompt",
    )(z, z, conv_w, conv_b.reshape(1, d), wa, wi, ba.reshape(1, d), bi.reshape(1, d), lam.reshape(1, d))


def _rg_sample_kernel(xr_ref, g_ref, sc_ref, h0_ref, cw_ref, cb_ref, wa_ref, wi_ref, ba_ref, bi_ref,
                      lam_ref, y_ref, conv_ref, h_ref):
    d = xr_ref.shape[1]
    x = xr_ref[...]
    cw = cw_ref[...]
    xc = cb_ref[...] + cw[0:1] * sc_ref[:, 0:d]
    xc = xc + cw[1:2] * sc_ref[:, d:2 * d]
    xc = xc + cw[2:3] * sc_ref[:, 2 * d:3 * d]
    xc = xc + cw[3:4] * x
    a, u = _rg_gates(xc, wa_ref, wi_ref, ba_ref, bi_ref, lam_ref)
    h = a * h0_ref[...] + u
    h_ref[...] = h
    y_ref[...] = (h * _silu(g_ref[...])).astype(y_ref.dtype)
    conv_ref[:, 0:d] = sc_ref[:, d:2 * d]
    conv_ref[:, d:2 * d] = sc_ref[:, 2 * d:3 * d]
    conv_ref[:, 2 * d:3 * d] = x


def _rg_sample(z, state_conv, h0, conv_w, conv_b, wa, wi, ba, bi, lam):
    b = z.shape[0]
    d = conv_w.shape[1]
    nprev = CONV_W - 1
    fix2 = lambda i: (0, 0)
    fix3 = lambda i: (0, 0, 0)
    return pl.pallas_call(
        _rg_sample_kernel,
        grid=(1,),
        in_specs=[pl.BlockSpec((b, d), lambda i: (0, Z_XR // d)),
                  pl.BlockSpec((b, d), lambda i: (0, Z_GRG // d)),
                  pl.BlockSpec((b, nprev * d), fix2), pl.BlockSpec((b, d), fix2),
                  pl.BlockSpec((CONV_W, d), fix2), pl.BlockSpec((1, d), fix2),
                  pl.BlockSpec(wa.shape, fix3), pl.BlockSpec(wi.shape, fix3),
                  pl.BlockSpec((1, d), fix2), pl.BlockSpec((1, d), fix2), pl.BlockSpec((1, d), fix2)],
        out_specs=[pl.BlockSpec((b, d), fix2), pl.BlockSpec((b, nprev * d), fix2),
                   pl.BlockSpec((b, d), fix2)],
        out_shape=[jax.ShapeDtypeStruct((b, d), BF), jax.ShapeDtypeStruct((b, nprev * d), F32),
                   jax.ShapeDtypeStruct((b, d), F32)],
        compiler_params=_cp("arbitrary"),
        name="rg_sample",
    )(z, z, state_conv.reshape(b, nprev * d), h0, conv_w, conv_b.reshape(1, d), wa, wi,
      ba.reshape(1, d), bi.reshape(1, d), lam.reshape(1, d))


def _memattn_kernel(q_ref, g_ref, k_ref, v_ref, o_ref):
    scale = MEM_DIM ** -0.5
    for h in range(MEM_HEADS):
        hs = slice(h * MEM_DIM, (h + 1) * MEM_DIM)
        s = _dot_t(q_ref[:, hs].astype(BF), k_ref[:, hs].astype(BF)) * scale
        p = jnp.exp(s - jnp.max(s, axis=1, keepdims=True))
        o = _dot(p.astype(BF), v_ref[:, hs].astype(BF)) / jnp.sum(p, axis=1, keepdims=True)
        o_ref[:, hs] = (o * _silu(g_ref[:, hs])).astype(o_ref.dtype)


def _memattn(z, k, v, tm):
    s = z.shape[0]
    d = MEM_HEADS * MEM_DIM
    nm = k.shape[0]
    return pl.pallas_call(
        _memattn_kernel,
        grid=(s // tm,),
        in_specs=[pl.BlockSpec((tm, d), lambda i: (i, Z_QMEM // d)),
                  pl.BlockSpec((tm, d), lambda i: (i, Z_GMEM // d)),
                  pl.BlockSpec((nm, d), lambda i: (0, 0)), pl.BlockSpec((nm, d), lambda i: (0, 0))],
        out_specs=pl.BlockSpec((tm, d), lambda i: (i, 0)),
        out_shape=jax.ShapeDtypeStruct((s, d), BF),
        compiler_params=_cp("parallel"),
        name="memattn",
    )(z, z, k, v)


def _memattn_s_kernel(q_ref, g_ref, k_ref, v_ref, o_ref, *, bt):
    scale = MEM_DIM ** -0.5
    d = MEM_HEADS * MEM_DIM
    lane = lax.broadcasted_iota(jnp.int32, (8, d), 1)
    rowi = lax.broadcasted_iota(jnp.int32, (8, d), 0)
    own = (lane // MEM_DIM) == rowi
    for b in range(bt):
        qb = jnp.broadcast_to(q_ref[b:b + 1, :], (8, d))
        qm = jnp.where(own, qb, 0.0).astype(BF)
        s = _dot_t(qm, k_ref[b].astype(BF)) * scale
        p = jnp.exp(s - jnp.max(s, axis=1, keepdims=True))
        o = _dot(p.astype(BF), v_ref[b].astype(BF)) / jnp.sum(p, axis=1, keepdims=True)
        ob = jnp.sum(jnp.where(own, o, 0.0), axis=0, keepdims=True)
        o_ref[b:b + 1, :] = ob * _silu(g_ref[b:b + 1, :])


def _memattn_s(z, k, v, bt):
    b, nm, d = k.shape
    return pl.pallas_call(
        functools.partial(_memattn_s_kernel, bt=bt),
        grid=(b // bt,),
        in_specs=[pl.BlockSpec((bt, d), lambda i: (i, Z_QMEM // d)),
                  pl.BlockSpec((bt, d), lambda i: (i, Z_GMEM // d)),
                  pl.BlockSpec((bt, nm, d), lambda i: (i, 0, 0)),
                  pl.BlockSpec((bt, nm, d), lambda i: (i, 0, 0))],
        out_specs=pl.BlockSpec((bt, d), lambda i: (i, 0)),
        out_shape=jax.ShapeDtypeStruct((b, d), F32),
        compiler_params=_cp("parallel"),
        name="memattn_s",
    )(z, z, k, v)


def _merge1_kernel(a_ref, b_ref, c_ref, wa_ref, wb_ref, wc_ref, ma_ref, mb_ref, mc_ref, o_ref):
    sg = jax.nn.sigmoid
    o = sg(ma_ref[...]) * _dot(a_ref[...].astype(BF), wa_ref[...])
    o = o + sg(mb_ref[...]) * _dot(b_ref[...].astype(BF), wb_ref[...])
    o = o + sg(mc_ref[...]) * _dot(c_ref[...].astype(BF), wc_ref[...])
    o_ref[...] = o.astype(o_ref.dtype)


def _merge1(a, b, c, wa, wb, wc, z, tm, tn):
    m = a.shape[0]
    n = wa.shape[1]
    row = lambda i, j: (i, 0)
    col = lambda i, j: (0, j)
    zcol = lambda off: (lambda i, j: (i, off // tn + j))
    return pl.pallas_call(
        _merge1_kernel,
        grid=(m // tm, n // tn),
        in_specs=[pl.BlockSpec((tm, a.shape[1]), row), pl.BlockSpec((tm, b.shape[1]), row),
                  pl.BlockSpec((tm, c.shape[1]), row),
                  pl.BlockSpec((wa.shape[0], tn), col), pl.BlockSpec((wb.shape[0], tn), col),
                  pl.BlockSpec((wc.shape[0], tn), col),
                  pl.BlockSpec((tm, tn), zcol(Z_MRG)), pl.BlockSpec((tm, tn), zcol(Z_MMLA)),
                  pl.BlockSpec((tm, tn), zcol(Z_MMEM))],
        out_specs=pl.BlockSpec((tm, tn), lambda i, j: (i, j)),
        out_shape=jax.ShapeDtypeStruct((m, n), BF),
        compiler_params=_cp("parallel", "parallel"),
        name="merge1",
    )(a, b, c, wa, wb, wc, z, z, z)


def _merge2_kernel(z_ref, w_ref, x_ref, g_ref, o_ref):
    o_ref[...] = _rms(x_ref[...] + _dot(z_ref[...], w_ref[...]), g_ref[...])


def _merge2(zz, w, x, g, tm):
    m, d = x.shape
    return pl.pallas_call(
        _merge2_kernel,
        grid=(m // tm,),
        in_specs=[pl.BlockSpec((tm, d), lambda i: (i, 0)), pl.BlockSpec(w.shape, lambda i: (0, 0)),
                  pl.BlockSpec((tm, d), lambda i: (i, 0)), pl.BlockSpec((1, d), lambda i: (0, 0))],
        out_specs=pl.BlockSpec((tm, d), lambda i: (i, 0)),
        out_shape=jax.ShapeDtypeStruct((m, d), F32),
        compiler_params=_cp("parallel"),
        name="merge2",
    )(zz, w, x, g.reshape(1, d))


def _rope_tables(pos):
    inv = ROPE_THETA ** (-jnp.arange(0, QK_ROPE, 2, dtype=F32) / QK_ROPE)
    ang = pos.astype(F32)[:, None] * inv[None, :]
    cos, sin = jnp.cos(ang), jnp.sin(ang)
    zero = jnp.zeros((pos.shape[0], LANES - QK_ROPE), F32)
    return (jnp.concatenate([cos, cos, zero], axis=1),
            jnp.concatenate([-sin, sin, zero], axis=1))


def kernel(x_prompt, x_sample, mem_prompt, cache_ckv, cache_krope, cache_mem_k, cache_mem_v, state_conv, state_rglru, page_table, norm_in, w_in, conv_w, conv_b, rg_wa, rg_ba, rg_wi, rg_bi, rg_lambda, kv_norm, w_uk, w_uv, mem_norm, w_mk, w_mv, w_rg_o, w_mla_o, w_mem_o, w_out, final_norm):
    bp, seq, d_model = x_prompt.shape
    bd, dec_seq, _ = x_sample.shape
    assert bp == 1 and dec_seq == 1
    d_rnn = conv_w.shape[1]
    n_pages = page_table.shape[1]
    past_len = n_pages * PAGE_SIZE
    half = QK_ROPE // 2

    d_q = N_HEADS * (QK_NOPE + QK_ROPE)
    d_v = N_HEADS * V_DIM
    d_m = MEM_HEADS * MEM_DIM
    o = [0]
    for sz in (d_rnn, d_rnn, d_q, KV_RANK, QK_ROPE, d_v, d_m, d_m, d_model, d_model, d_model):
        o.append(o[-1] + sz)
    seg = lambda i: w_in[:, o[i]:o[i + 1]]
    w_main = jnp.concatenate([seg(0), seg(1), seg(5), seg(6), seg(7), seg(8), seg(9), seg(10)],
                             axis=1).astype(BF)
    wq3 = seg(2).reshape(d_model, N_HEADS, QK_NOPE + QK_ROPE)
    wq = jnp.concatenate([wq3[:, :, :QK_NOPE + QK_ROPE], wq3[:, :, QK_NOPE + half:],
                          wq3[:, :, QK_NOPE:QK_NOPE + half]], axis=2)
    wq = jnp.transpose(wq, (1, 0, 2)).astype(BF)
    wkr = seg(4)
    wkv = jnp.concatenate([seg(3), wkr, wkr[:, half:], wkr[:, :half]], axis=1).astype(BF)
    wuk2 = w_uk.reshape(KV_RANK, N_HEADS * QK_NOPE).astype(BF)
    wuv2 = w_uv.reshape(KV_RANK, d_v).astype(BF)
    wuk_t = jnp.transpose(w_uk, (1, 2, 0)).astype(BF)
    wuv_t = jnp.transpose(w_uv, (1, 0, 2)).astype(BF)
    w_rg_o_b, w_mla_o_b, w_mem_o_b, w_out_b = (w.astype(BF) for w in (w_rg_o, w_mla_o, w_mem_o, w_out))
    wa_b, wi_b = rg_wa.astype(BF), rg_wi.astype(BF)
    w_mkv = jnp.concatenate([w_mk, w_mv], axis=1).astype(BF)

    xp = x_prompt.reshape(seq, d_model)
    xs = x_sample.reshape(bd, d_model)

    cos_p, sin_p = _rope_tables(jnp.arange(seq))
    xn_p = _norm_cast(xp, norm_in, 512)
    z_p = _matmul(xn_p, w_main, 1024, 1024, F32, "inproj")
    q_p = _qproj(xn_p, wq, cos_p, sin_p, 1024)
    ckv_p, kr_p, ckvb_p, krb_p = _kvproj(xn_p, wkv, cos_p, sin_p, kv_norm, 512)
    k_p, v_p = _kvup(ckvb_p, krb_p, wuk2, wuv2, 512)
    b_p = _flash(q_p, k_p, v_p, z_p, 512)
    a_p, conv_p, h_p = _rg_prompt(z_p, conv_w, conv_b, wa_b, wi_b, rg_ba, rg_bi, rg_lambda, 512)
    mn = _norm_cast(mem_prompt.reshape(-1, d_model), mem_norm, 256)
    mkv = _matmul(mn, w_mkv, 256, 1024, F32, "memkv")
    mem_k, mem_v = mkv[:, :d_m], mkv[:, d_m:]
    c_p = _memattn(z_p, mem_k, mem_v, 512)
    zz_p = _merge1(a_p, b_p, c_p, w_rg_o_b, w_mla_o_b, w_mem_o_b, z_p, 512, 512)
    y_p = _merge2(zz_p, w_out_b, xp, final_norm, 512)

    cos_s, sin_s = _rope_tables(jnp.full((bd,), past_len))
    xn_s = _norm_cast(xs, norm_in, bd)
    z_s = _matmul(xn_s, w_main, bd, 1024, F32, "inproj_s")
    q_s = _qproj(xn_s, wq, cos_s, sin_s, bd)
    ckv_s, kr_s, ckvb_s, krb_s = _kvproj(xn_s, wkv, cos_s, sin_s, kv_norm, bd)
    qcat = jnp.transpose(_qlat(q_s, wuk_t), (1, 0, 2))
    lat = _decode(page_table, qcat, ckvb_s, krb_s, cache_ckv, jnp.transpose(cache_krope, (0, 2, 1)), 32)
    b_s = _uvproj(lat, wuv_t, z_s)
    a_s, conv_s, h_s = _rg_sample(z_s, state_conv, state_rglru, conv_w, conv_b, wa_b, wi_b,
                                  rg_ba, rg_bi, rg_lambda)
    c_s = _memattn_s(z_s, cache_mem_k.reshape(bd, -1, d_m), cache_mem_v.reshape(bd, -1, d_m), 8)
    zz_s = _merge1(a_s, b_s, c_s, w_rg_o_b, w_mla_o_b, w_mem_o_b, z_s, bd, 512)
    y_s = _merge2(zz_s, w_out_b, xs, final_norm, bd)

    n_mem = mem_prompt.shape[1]
    return (y_p.reshape(1, seq, d_model), y_s.reshape(bd, 1, d_model),
            ckv_p.reshape(1, seq, KV_RANK), kr_p.reshape(1, seq, QK_ROPE),
            conv_p.reshape(1, CONV_W - 1, d_rnn), h_p.reshape(1, d_rnn),
            mem_k.reshape(1, n_mem, MEM_HEADS, MEM_DIM), mem_v.reshape(1, n_mem, MEM_HEADS, MEM_DIM),
            ckv_s.reshape(bd, 1, KV_RANK), kr_s.reshape(bd, 1, QK_ROPE),
            conv_s.reshape(bd, CONV_W - 1, d_rnn), h_s)
```

```python
import functools
import math

import jax
import jax.numpy as jnp
from jax import lax
from jax.experimental import pallas as pl
from jax.experimental.pallas import tpu as pltpu

F32 = jnp.float32
BF = jnp.bfloat16

EPS = 1e-6
RG_BLOCKS = 8
RG_BLOCK = 128
CONV_W = 4
RG_C = 8.0
N_HEADS = 16
QK_NOPE = 128
QK_ROPE = 64
V_DIM = 128
KV_RANK = 512
ROPE_THETA = 10000.0
MEM_HEADS = 4
MEM_DIM = 256
PAGE_SIZE = 128
LANES = 128
HEAD_PAD = 256
LAT_PAD = KV_RANK + LANES
VMEM_LIMIT = 48 * 1024 * 1024
QK_SCALE2 = (QK_NOPE + QK_ROPE) ** -0.5 * math.log2(math.e)

Z_XR, Z_GRG, Z_GMLA, Z_QMEM, Z_GMEM, Z_MRG, Z_MMLA, Z_MMEM, Z_COLS = (
    0, 1024, 2048, 4096, 5120, 6144, 8192, 10240, 12288)
W_Q_ROW, W_KV_ROW, W_GMLA_ROW = 2048, 5120, 5696
ROW_ALIGN = 64


def _cp(*sem):
    return pltpu.CompilerParams(dimension_semantics=sem, vmem_limit_bytes=VMEM_LIMIT)


def _silu(g):
    return g * jax.nn.sigmoid(g)


def _rms(x, g):
    return x * lax.rsqrt(jnp.mean(x * x, axis=-1, keepdims=True) + EPS) * g


def _dot(a, b):
    return jnp.dot(a, b, preferred_element_type=F32)


def _dot_t(a, b):
    return lax.dot_general(a, b, (((1,), (1,)), ((), ())), preferred_element_type=F32)


def _norm_cast_kernel(x_ref, g_ref, o_ref):
    o_ref[...] = _rms(x_ref[...], g_ref[...]).astype(o_ref.dtype)


def _norm_cast(x, g, tm):
    m, d = x.shape
    return pl.pallas_call(
        _norm_cast_kernel,
        grid=(m // tm,),
        in_specs=[pl.BlockSpec((tm, d), lambda i: (i, 0)),
                  pl.BlockSpec((1, d), lambda i: (0, 0))],
        out_specs=pl.BlockSpec((tm, d), lambda i: (i, 0)),
        out_shape=jax.ShapeDtypeStruct((m, d), BF),
        compiler_params=_cp("parallel"),
        name="norm_cast",
    )(x, g.reshape(1, d))


def _mm_kernel(a_ref, w_ref, o_ref):
    o_ref[...] = _dot(a_ref[...], w_ref[...]).astype(o_ref.dtype)


def _matmul(a, w, tm, tn, out_dtype, name):
    m, k = a.shape
    n = w.shape[1]
    return pl.pallas_call(
        _mm_kernel,
        grid=(m // tm, n // tn),
        in_specs=[pl.BlockSpec((tm, k), lambda i, j: (i, 0)),
                  pl.BlockSpec((k, tn), lambda i, j: (0, j))],
        out_specs=pl.BlockSpec((tm, tn), lambda i, j: (i, j)),
        out_shape=jax.ShapeDtypeStruct((m, n), out_dtype),
        compiler_params=_cp("parallel", "parallel"),
        name=name,
    )(a, w)


def _mmt_kernel(a_ref, w_ref, o_ref):
    o_ref[...] = _dot_t(a_ref[...], w_ref[...]).astype(o_ref.dtype)


def _inproj(xn, wt, tm, tn, name):
    m, k = xn.shape
    assert W_Q_ROW % tn == 0
    skipped = W_GMLA_ROW - W_Q_ROW
    assert skipped % ROW_ALIGN == 0 and tn % ROW_ALIGN == 0
    wrow = lambda i, j: (pl.multiple_of(j * tn + jnp.where(j * tn < W_Q_ROW, 0, skipped), ROW_ALIGN), 0)
    return pl.pallas_call(
        _mmt_kernel,
        grid=(m // tm, Z_COLS // tn),
        in_specs=[pl.BlockSpec((tm, k), lambda i, j: (i, 0)),
                  pl.BlockSpec((pl.Element(tn), pl.Element(k)), wrow)],
        out_specs=pl.BlockSpec((tm, tn), lambda i, j: (i, j)),
        out_shape=jax.ShapeDtypeStruct((m, Z_COLS), F32),
        compiler_params=_cp("parallel", "parallel"),
        name=name,
    )(xn, wt)


def _rope_hi(hi, c, s):
    half = QK_ROPE // 2
    lane = lax.broadcasted_iota(jnp.int32, hi.shape, 1)
    swapped = jnp.where(lane < half, pltpu.roll(hi, LANES - half, 1), pltpu.roll(hi, half, 1))
    return hi * c + swapped * s


def _qproj_kernel(a_ref, w_ref, c_ref, s_ref, o_ref):
    res = _dot_t(a_ref[...], w_ref[...]) * QK_SCALE2
    o_ref[0, :, :QK_NOPE] = res[:, :QK_NOPE].astype(BF)
    o_ref[0, :, QK_NOPE:] = _rope_hi(res[:, QK_NOPE:], c_ref[...], s_ref[...]).astype(BF)


def _qproj(xn, wt, cos_t, sin_t, tm):
    m, d = xn.shape
    return pl.pallas_call(
        _qproj_kernel,
        grid=(m // tm, N_HEADS),
        in_specs=[pl.BlockSpec((tm, d), lambda i, h: (i, 0)),
                  pl.BlockSpec((pl.Element(HEAD_PAD), pl.Element(d)),
                               lambda i, h: (pl.multiple_of(W_Q_ROW + h * (QK_NOPE + QK_ROPE), ROW_ALIGN), 0)),
                  pl.BlockSpec((tm, LANES), lambda i, h: (i, 0)),
                  pl.BlockSpec((tm, LANES), lambda i, h: (i, 0))],
        out_specs=pl.BlockSpec((1, tm, HEAD_PAD), lambda i, h: (h, i, 0)),
        out_shape=jax.ShapeDtypeStruct((N_HEADS, m, HEAD_PAD), BF),
        compiler_params=_cp("parallel", "parallel"),
        name="qproj",
    )(xn, wt, cos_t, sin_t)


def _kvproj_kernel(a_ref, w_ref, c_ref, s_ref, g_ref, ckv_ref, kr_ref, ckvb_ref, krb_ref):
    res = _dot_t(a_ref[...], w_ref[...])
    ckv = _rms(res[:, :KV_RANK], g_ref[...])
    ckv_ref[...] = ckv
    ckvb_ref[...] = ckv.astype(BF)
    rot = _rope_hi(res[:, KV_RANK:], c_ref[...], s_ref[...])
    kr_ref[...] = rot[:, :QK_ROPE]
    krb_ref[...] = rot.astype(BF)


def _kvproj(xn, wt, cos_t, sin_t, kv_norm, tm):
    m, d = xn.shape
    row = lambda i: (i, 0)
    fix = lambda i: (0, 0)
    return pl.pallas_call(
        _kvproj_kernel,
        grid=(m // tm,),
        in_specs=[pl.BlockSpec((tm, d), row),
                  pl.BlockSpec((pl.Element(KV_RANK + LANES), pl.Element(d)), lambda i: (W_KV_ROW, 0)),
                  pl.BlockSpec((tm, LANES), row), pl.BlockSpec((tm, LANES), row),
                  pl.BlockSpec((1, KV_RANK), fix)],
        out_specs=[pl.BlockSpec((tm, KV_RANK), row), pl.BlockSpec((tm, QK_ROPE), row),
                   pl.BlockSpec((tm, KV_RANK), row), pl.BlockSpec((tm, LANES), row)],
        out_shape=[jax.ShapeDtypeStruct((m, KV_RANK), F32), jax.ShapeDtypeStruct((m, QK_ROPE), F32),
                   jax.ShapeDtypeStruct((m, KV_RANK), BF), jax.ShapeDtypeStruct((m, LANES), BF)],
        compiler_params=_cp("parallel"),
        name="kvproj",
    )(xn, wt, cos_t, sin_t, kv_norm.reshape(1, KV_RANK))


def _kvup_kernel(c_ref, kr_ref, wk_ref, wv_ref, k_ref, v_ref):
    c = c_ref[...]
    kn = _dot(c, wk_ref[...])
    v = _dot(c, wv_ref[...])
    kr = kr_ref[...]
    ones = jnp.ones((c.shape[0], V_DIM), BF)
    for h in range(N_HEADS):
        k_ref[h, :, :QK_NOPE] = kn[:, h * QK_NOPE:(h + 1) * QK_NOPE].astype(BF)
        k_ref[h, :, QK_NOPE:] = kr
        v_ref[h, :, :V_DIM] = v[:, h * V_DIM:(h + 1) * V_DIM].astype(BF)
        v_ref[h, :, V_DIM:] = ones


def _kvup(ckvb, krb, wuk, wuv, tm):
    s = ckvb.shape[0]
    row = lambda i: (i, 0)
    fix = lambda i: (0, 0)
    return pl.pallas_call(
        _kvup_kernel,
        grid=(s // tm,),
        in_specs=[pl.BlockSpec((tm, KV_RANK), row), pl.BlockSpec((tm, LANES), row),
                  pl.BlockSpec(wuk.shape, fix), pl.BlockSpec(wuv.shape, fix)],
        out_specs=[pl.BlockSpec((N_HEADS, tm, HEAD_PAD), lambda i: (0, i, 0)),
                   pl.BlockSpec((N_HEADS, tm, 2 * V_DIM), lambda i: (0, i, 0))],
        out_shape=[jax.ShapeDtypeStruct((N_HEADS, s, HEAD_PAD), BF),
                   jax.ShapeDtypeStruct((N_HEADS, s, 2 * V_DIM), BF)],
        compiler_params=_cp("parallel"),
        name="kvup",
    )(ckvb, krb, wuk, wuv)


def _flash_kernel(q_ref, k_ref, v_ref, g_ref, o_ref, sa, sb, mxa, mxb, m_sc, acc_sc, *, tq):
    qi = pl.program_id(1)
    q = q_ref[0]
    m_sc[...] = jnp.full(m_sc.shape, -jnp.inf, F32)
    acc_sc[...] = jnp.zeros(acc_sc.shape, F32)

    def stage_x(t, s_buf, mx_buf):
        start = pl.multiple_of(t * tq, tq)
        s = _dot_t(q, k_ref[0, pl.ds(start, tq), :])
        s_buf[...] = s
        mx_buf[...] = jnp.broadcast_to(jnp.max(s, axis=1, keepdims=True), mx_buf.shape)

    def stage_y(t, s_buf, mx_buf, masked):
        s = s_buf[...]
        if masked:
            row = lax.broadcasted_iota(jnp.int32, (tq, tq), 0)
            col = lax.broadcasted_iota(jnp.int32, (tq, tq), 1)
            s = jnp.where(col <= row, s, -jnp.inf)
            mx = jnp.max(s, axis=1, keepdims=True)
        else:
            mx = mx_buf[...]
        m_prev = m_sc[...]
        m_new = jnp.maximum(m_prev, mx)
        m_sc[...] = m_new
        p = jnp.exp2(s - jnp.tile(m_new, (1, tq // LANES))).astype(BF)
        alpha = jnp.exp2(m_prev - m_new)
        start = pl.multiple_of(t * tq, tq)
        acc_sc[...] = jnp.tile(alpha, (1, 2)) * acc_sc[...] + _dot(p, v_ref[0, pl.ds(start, tq), :])

    stage_x(0, sa, mxa)

    def pair(i, carry):
        t = 2 * i
        stage_x(t + 1, sb, mxb)
        stage_y(t, sa, mxa, False)
        stage_x(t + 2, sa, mxa)
        stage_y(t + 1, sb, mxb, False)
        return carry

    lax.fori_loop(0, qi // 2, pair, 0)

    @pl.when(qi % 2 == 1)
    def _():
        stage_x(qi, sb, mxb)
        stage_y(qi - 1, sa, mxa, False)
        stage_y(qi, sb, mxb, True)

    @pl.when(qi % 2 == 0)
    def _():
        stage_y(qi, sa, mxa, True)

    o = acc_sc[:, :V_DIM] / acc_sc[:, V_DIM:]
    o_ref[...] = (o * _silu(g_ref[...])).astype(o_ref.dtype)


def _flash(q, k, v, z, tq):
    h, s, _ = q.shape
    gcol = Z_GMLA // V_DIM
    return pl.pallas_call(
        functools.partial(_flash_kernel, tq=tq),
        grid=(h, s // tq),
        in_specs=[pl.BlockSpec((1, tq, HEAD_PAD), lambda hh, i: (hh, i, 0)),
                  pl.BlockSpec((1, s, HEAD_PAD), lambda hh, i: (hh, 0, 0)),
                  pl.BlockSpec((1, s, 2 * V_DIM), lambda hh, i: (hh, 0, 0)),
                  pl.BlockSpec((tq, V_DIM), lambda hh, i: (i, gcol + hh))],
        out_specs=pl.BlockSpec((tq, V_DIM), lambda hh, i: (i, hh)),
        out_shape=jax.ShapeDtypeStruct((s, h * V_DIM), BF),
        scratch_shapes=[pltpu.VMEM((tq, tq), F32), pltpu.VMEM((tq, tq), F32),
                        pltpu.VMEM((tq, LANES), F32), pltpu.VMEM((tq, LANES), F32),
                        pltpu.VMEM((tq, LANES), F32), pltpu.VMEM((tq, 2 * V_DIM), F32)],
        compiler_params=_cp("parallel", "arbitrary"),
        name="flash",
    )(q, k, v, z)


def _decode_kernel(pt_ref, q_ref, cn_ref, kn_ref, ckv_hbm, kr_hbm, o_ref, cbuf, rbuf, sem, *, npg, nchunk):
    b = pl.program_id(0)
    nb = pl.num_programs(0)

    def copies(seq, j, slot, wait_only=False):
        cps = []
        for p in range(npg):
            page = 0 if wait_only else pt_ref[(seq * nchunk + j) * npg + p]
            keys = pl.ds(p * PAGE_SIZE, PAGE_SIZE)
            cps.append(pltpu.make_async_copy(ckv_hbm.at[page], cbuf.at[slot, keys, :], sem.at[0, slot]))
            cps.append(pltpu.make_async_copy(kr_hbm.at[page], rbuf.at[slot, :, keys], sem.at[1, slot]))
        return cps

    @pl.when(b == 0)
    def _():
        for cp in copies(0, 0, 0):
            cp.start()

    q = q_ref[0]
    ql = q[:, :KV_RANK]
    qr = q[:, KV_RANK:KV_RANK + QK_ROPE]
    m = jnp.full((N_HEADS, 1), -jnp.inf, F32)
    l = jnp.zeros((N_HEADS, 1), F32)
    acc = jnp.zeros((N_HEADS, KV_RANK), F32)
    for j in range(nchunk):
        slot = j % 2
        if j + 1 < nchunk:
            for cp in copies(b, j + 1, 1 - slot):
                cp.start()
        else:
            @pl.when(b + 1 < nb)
            def _():
                for cp in copies(b + 1, 0, 1 - slot):
                    cp.start()
        for cp in copies(b, j, slot, wait_only=True):
            cp.wait()
        cb = cbuf[slot].astype(BF)
        s = _dot_t(ql, cb) + _dot(qr, rbuf[slot].astype(BF))
        m_new = jnp.maximum(m, jnp.max(s, axis=1, keepdims=True))
        alpha = jnp.exp2(m - m_new)
        p = jnp.exp2(s - m_new)
        l = alpha * l + jnp.sum(p, axis=1, keepdims=True)
        acc = alpha * acc + _dot(p.astype(BF), cb)
        m = m_new

    qf = q.astype(F32)
    cn = cn_ref[0].astype(F32)
    kn = kn_ref[0].astype(F32)
    s_new = (jnp.sum(qf[:, :KV_RANK] * cn, axis=1, keepdims=True)
             + jnp.sum(qf[:, KV_RANK:] * kn, axis=1, keepdims=True))
    m_fin = jnp.maximum(m, s_new)
    a_fin = jnp.exp2(m - m_fin)
    p_new = jnp.exp2(s_new - m_fin)
    l_fin = a_fin * l + p_new
    out = (a_fin * acc + p_new.astype(BF).astype(F32) * cn) / l_fin
    for h in range(N_HEADS):
        o_ref[0, :, h * KV_RANK:(h + 1) * KV_RANK] = out[h:h + 1, :]


def _decode(page_table, qcat, ckvb, krb, cache_ckv, cache_krope_t, npg):
    b, n_pages = page_table.shape
    nchunk = n_pages // npg
    assert nchunk * npg == n_pages and nchunk % 2 == 0
    per_b = lambda bb, pt: (bb, 0, 0)
    grid_spec = pltpu.PrefetchScalarGridSpec(
        num_scalar_prefetch=1,
        grid=(b,),
        in_specs=[pl.BlockSpec((1, N_HEADS, LAT_PAD), per_b),
                  pl.BlockSpec((1, 1, KV_RANK), per_b),
                  pl.BlockSpec((1, 1, LANES), per_b),
                  pl.BlockSpec(memory_space=pl.ANY),
                  pl.BlockSpec(memory_space=pl.ANY)],
        out_specs=pl.BlockSpec((1, 1, N_HEADS * KV_RANK), per_b),
        scratch_shapes=[pltpu.VMEM((2, npg * PAGE_SIZE, KV_RANK), F32),
                        pltpu.VMEM((2, QK_ROPE, npg * PAGE_SIZE), F32),
                        pltpu.SemaphoreType.DMA((2, 2))])
    out = pl.pallas_call(
        functools.partial(_decode_kernel, npg=npg, nchunk=nchunk),
        grid_spec=grid_spec,
        out_shape=jax.ShapeDtypeStruct((b, 1, N_HEADS * KV_RANK), F32),
        compiler_params=_cp("arbitrary"),
        name="decode",
    )(page_table.reshape(-1), qcat, ckvb.reshape(b, 1, KV_RANK), krb.reshape(b, 1, LANES),
      cache_ckv, cache_krope_t)
    return out.reshape(b, N_HEADS * KV_RANK)


def _qlat_kernel(q_ref, w_ref, o_ref):
    q = q_ref[0]
    o_ref[0, :, :KV_RANK] = _dot(q[:, :QK_NOPE], w_ref[0]).astype(BF)
    o_ref[0, :, KV_RANK:] = q[:, QK_NOPE:]


def _qlat(qs, wuk_t):
    h, b, _ = qs.shape
    return pl.pallas_call(
        _qlat_kernel,
        grid=(h,),
        in_specs=[pl.BlockSpec((1, b, HEAD_PAD), lambda i: (i, 0, 0)),
                  pl.BlockSpec((1, QK_NOPE, KV_RANK), lambda i: (i, 0, 0))],
        out_specs=pl.BlockSpec((1, b, LAT_PAD), lambda i: (i, 0, 0)),
        out_shape=jax.ShapeDtypeStruct((h, b, LAT_PAD), BF),
        compiler_params=_cp("parallel"),
        name="qlat",
    )(qs, wuk_t)


def _uvproj_kernel(l_ref, w_ref, g_ref, o_ref):
    o = _dot(l_ref[...].astype(BF), w_ref[0])
    o_ref[...] = (o * _silu(g_ref[...])).astype(o_ref.dtype)


def _uvproj(lat, wuv_t, z):
    b = lat.shape[0]
    gcol = Z_GMLA // V_DIM
    return pl.pallas_call(
        _uvproj_kernel,
        grid=(N_HEADS,),
        in_specs=[pl.BlockSpec((b, KV_RANK), lambda h: (0, h)),
                  pl.BlockSpec((1, KV_RANK, V_DIM), lambda h: (h, 0, 0)),
                  pl.BlockSpec((b, V_DIM), lambda h: (0, gcol + h))],
        out_specs=pl.BlockSpec((b, V_DIM), lambda h: (0, h)),
        out_shape=jax.ShapeDtypeStruct((b, N_HEADS * V_DIM), BF),
        compiler_params=_cp("parallel"),
        name="uvproj",
    )(lat, wuv_t, z)


def _rg_gates(xc, wa_ref, wi_ref, ba_ref, bi_ref, lam_ref):
    xcb = xc.astype(BF)
    blk = lambda n: slice(n * RG_BLOCK, (n + 1) * RG_BLOCK)
    ra = jnp.concatenate([_dot(xcb[:, blk(n)], wa_ref[n]) for n in range(RG_BLOCKS)], axis=1)
    ri = jnp.concatenate([_dot(xcb[:, blk(n)], wi_ref[n]) for n in range(RG_BLOCKS)], axis=1)
    r = jax.nn.sigmoid(ra + ba_ref[...])
    i = jax.nn.sigmoid(ri + bi_ref[...])
    log_a = -RG_C * r * jax.nn.softplus(-lam_ref[...])
    a = jnp.exp(log_a)
    u = jnp.sqrt(-_expm1(2.0 * log_a)) * (i * xc)
    return a, u


def _expm1(x):
    e = jnp.exp(x)
    em = e - 1.0
    ok = jnp.logical_and(em != 0.0, e > 0.0)
    kahan = em * x / jnp.log(jnp.where(ok, e, 2.0))
    return jnp.where(ok, kahan, jnp.where(em == 0.0, x, em))


def _rg_kernel(xr_ref, g_ref, cw_ref, cb_ref, wa_ref, wi_ref, ba_ref, bi_ref, lam_ref,
               y_ref, conv_ref, ht_ref, xp_sc, h_sc, a_sc, u_sc, *, tt):
    t = pl.program_id(0)
    pad = 8
    d = xr_ref.shape[1]

    @pl.when(t == 0)
    def _():
        xp_sc[0:pad, :] = jnp.zeros((pad, d), F32)
        h_sc[...] = jnp.zeros(h_sc.shape, F32)

    @pl.when(t > 0)
    def _():
        xp_sc[0:pad, :] = xp_sc[tt:tt + pad, :]

    x = xr_ref[...]
    xp_sc[pad:pad + tt, :] = x
    cw = cw_ref[...]
    xc = cb_ref[...] + cw[0:1] * xp_sc[pad - 3:pad - 3 + tt, :]
    xc = xc + cw[1:2] * xp_sc[pad - 2:pad - 2 + tt, :]
    xc = xc + cw[2:3] * xp_sc[pad - 1:pad - 1 + tt, :]
    xc = xc + cw[3:4] * x
    a, u = _rg_gates(xc, wa_ref, wi_ref, ba_ref, bi_ref, lam_ref)
    a_sc[...] = a
    u_sc[...] = u

    row = lax.broadcasted_iota(jnp.int32, (8, d), 0)

    def group(gi, h):
        s0 = pl.multiple_of(gi * 8, 8)
        aa = a_sc[pl.ds(s0, 8), :]
        uu = u_sc[pl.ds(s0, 8), :]
        for sh in (1, 2, 4):
            keep = row >= sh
            uu = jnp.where(keep, uu + aa * pltpu.roll(uu, sh, 0), uu)
            aa = jnp.where(keep, aa * pltpu.roll(aa, sh, 0), aa)
        hs = aa * h + uu
        u_sc[pl.ds(s0, 8), :] = hs
        return jnp.broadcast_to(hs[7:8, :], (8, d))

    h = lax.fori_loop(0, tt // 8, group, h_sc[...])
    h_sc[...] = h
    y_ref[...] = (u_sc[...] * _silu(g_ref[...])).astype(y_ref.dtype)

    @pl.when(t == pl.num_programs(0) - 1)
    def _():
        conv_ref[...] = xp_sc[pad + tt - (CONV_W - 1):pad + tt, :]
        ht_ref[...] = h[0:1, :]


def _rg_prompt(z, conv_w, conv_b, wa, wi, ba, bi, lam, tt):
    s = z.shape[0]
    d = conv_w.shape[1]
    fix2 = lambda i: (0, 0)
    fix3 = lambda i: (0, 0, 0)
    return pl.pallas_call(
        functools.partial(_rg_kernel, tt=tt),
        grid=(s // tt,),
        in_specs=[pl.BlockSpec((tt, d), lambda i: (i, Z_XR // d)),
                  pl.BlockSpec((tt, d), lambda i: (i, Z_GRG // d)),
                  pl.BlockSpec((CONV_W, d), fix2), pl.BlockSpec((1, d), fix2),
                  pl.BlockSpec(wa.shape, fix3), pl.BlockSpec(wi.shape, fix3),
                  pl.BlockSpec((1, d), fix2), pl.BlockSpec((1, d), fix2), pl.BlockSpec((1, d), fix2)],
        out_specs=[pl.BlockSpec((tt, d), lambda i: (i, 0)),
                   pl.BlockSpec((CONV_W - 1, d), fix2), pl.BlockSpec((1, d), fix2)],
        out_shape=[jax.ShapeDtypeStruct((s, d), BF),
                   jax.ShapeDtypeStruct((CONV_W - 1, d), F32), jax.ShapeDtypeStruct((1, d), F32)],
        scratch_shapes=[pltpu.VMEM((tt + 8, d), F32), pltpu.VMEM((8, d), F32),
                        pltpu.VMEM((tt, d), F32), pltpu.VMEM((tt, d), F32)],
        compiler_params=_cp("arbitrary"),
        name="rg_prompt",
    )(z, z, conv_w, conv_b.reshape(1, d), wa, wi, ba.reshape(1, d), bi.reshape(1, d), lam.reshape(1, d))


def _rg_sample_kernel(xr_ref, g_ref, sc_ref, h0_ref, cw_ref, cb_ref, wa_ref, wi_ref, ba_ref, bi_ref,
                      lam_ref, y_ref, conv_ref, h_ref):
    d = xr_ref.shape[1]
    x = xr_ref[...]
    cw = cw_ref[...]
    xc = cb_ref[...] + cw[0:1] * sc_ref[:, 0:d]
    xc = xc + cw[1:2] * sc_ref[:, d:2 * d]
    xc = xc + cw[2:3] * sc_ref[:, 2 * d:3 * d]
    xc = xc + cw[3:4] * x
    a, u = _rg_gates(xc, wa_ref, wi_ref, ba_ref, bi_ref, lam_ref)
    h = a * h0_ref[...] + u
    h_ref[...] = h
    y_ref[...] = (h * _silu(g_ref[...])).astype(y_ref.dtype)
    conv_ref[:, 0:d] = sc_ref[:, d:2 * d]
    conv_ref[:, d:2 * d] = sc_ref[:, 2 * d:3 * d]
    conv_ref[:, 2 * d:3 * d] = x


def _rg_sample(z, state_conv, h0, conv_w, conv_b, wa, wi, ba, bi, lam):
    b = z.shape[0]
    d = conv_w.shape[1]
    nprev = CONV_W - 1
    fix2 = lambda i: (0, 0)
    fix3 = lambda i: (0, 0, 0)
    return pl.pallas_call(
        _rg_sample_kernel,
        grid=(1,),
        in_specs=[pl.BlockSpec((b, d), lambda i: (0, Z_XR // d)),
                  pl.BlockSpec((b, d), lambda i: (0, Z_GRG // d)),
                  pl.BlockSpec((b, nprev * d), fix2), pl.BlockSpec((b, d), fix2),
                  pl.BlockSpec((CONV_W, d), fix2), pl.BlockSpec((1, d), fix2),
                  pl.BlockSpec(wa.shape, fix3), pl.BlockSpec(wi.shape, fix3),
                  pl.BlockSpec((1, d), fix2), pl.BlockSpec((1, d), fix2), pl.BlockSpec((1, d), fix2)],
        out_specs=[pl.BlockSpec((b, d), fix2), pl.BlockSpec((b, nprev * d), fix2),
                   pl.BlockSpec((b, d), fix2)],
        out_shape=[jax.ShapeDtypeStruct((b, d), BF), jax.ShapeDtypeStruct((b, nprev * d), F32),
                   jax.ShapeDtypeStruct((b, d), F32)],
        compiler_params=_cp("arbitrary"),
        name="rg_sample",
    )(z, z, state_conv.reshape(b, nprev * d), h0, conv_w, conv_b.reshape(1, d), wa, wi,
      ba.reshape(1, d), bi.reshape(1, d), lam.reshape(1, d))


def _memattn_kernel(q_ref, g_ref, k_ref, v_ref, o_ref):
    scale = MEM_DIM ** -0.5
    for h in range(MEM_HEADS):
        hs = slice(h * MEM_DIM, (h + 1) * MEM_DIM)
        s = _dot_t(q_ref[:, hs].astype(BF), k_ref[:, hs].astype(BF)) * scale
        p = jnp.exp(s - jnp.max(s, axis=1, keepdims=True))
        o = _dot(p.astype(BF), v_ref[:, hs].astype(BF)) / jnp.sum(p, axis=1, keepdims=True)
        o_ref[:, hs] = (o * _silu(g_ref[:, hs])).astype(o_ref.dtype)


def _memattn(z, k, v, tm):
    s = z.shape[0]
    d = MEM_HEADS * MEM_DIM
    nm = k.shape[0]
    return pl.pallas_call(
        _memattn_kernel,
        grid=(s // tm,),
        in_specs=[pl.BlockSpec((tm, d), lambda i: (i, Z_QMEM // d)),
                  pl.BlockSpec((tm, d), lambda i: (i, Z_GMEM // d)),
                  pl.BlockSpec((nm, d), lambda i: (0, 0)), pl.BlockSpec((nm, d), lambda i: (0, 0))],
        out_specs=pl.BlockSpec((tm, d), lambda i: (i, 0)),
        out_shape=jax.ShapeDtypeStruct((s, d), BF),
        compiler_params=_cp("parallel"),
        name="memattn",
    )(z, z, k, v)


def _memattn_s_kernel(q_ref, g_ref, k_ref, v_ref, o_ref, *, bt):
    scale = MEM_DIM ** -0.5
    for b in range(bt):
        s = jnp.sum(k_ref[b] * q_ref[b][None], axis=-1, keepdims=True) * scale
        p = jnp.exp(s - jnp.max(s, axis=0, keepdims=True))
        o = jnp.sum(p * v_ref[b], axis=0) / jnp.sum(p, axis=0)
        o_ref[b] = o * _silu(g_ref[b])


def _memattn_s(q, g, k, v, bt):
    b, nm, nh, d = k.shape
    small = pl.BlockSpec((bt, nh, d), lambda i: (i, 0, 0))
    big = pl.BlockSpec((bt, nm, nh, d), lambda i: (i, 0, 0, 0))
    return pl.pallas_call(
        functools.partial(_memattn_s_kernel, bt=bt),
        grid=(b // bt,),
        in_specs=[small, small, big, big],
        out_specs=small,
        out_shape=jax.ShapeDtypeStruct((b, nh, d), F32),
        compiler_params=_cp("parallel"),
        name="memattn_s",
    )(q, g, k, v)


def _merge1_kernel(a_ref, b_ref, c_ref, wa_ref, wb_ref, wc_ref, ma_ref, mb_ref, mc_ref, o_ref):
    sg = jax.nn.sigmoid
    o = sg(ma_ref[...]) * _dot(a_ref[...].astype(BF), wa_ref[...])
    o = o + sg(mb_ref[...]) * _dot(b_ref[...].astype(BF), wb_ref[...])
    o = o + sg(mc_ref[...]) * _dot(c_ref[...].astype(BF), wc_ref[...])
    o_ref[...] = o.astype(o_ref.dtype)


def _merge1(a, b, c, wa, wb, wc, z, tm, tn):
    m = a.shape[0]
    n = wa.shape[1]
    row = lambda i, j: (i, 0)
    col = lambda i, j: (0, j)
    zcol = lambda off: (lambda i, j: (i, off // tn + j))
    return pl.pallas_call(
        _merge1_kernel,
        grid=(m // tm, n // tn),
        in_specs=[pl.BlockSpec((tm, a.shape[1]), row), pl.BlockSpec((tm, b.shape[1]), row),
                  pl.BlockSpec((tm, c.shape[1]), row),
                  pl.BlockSpec((wa.shape[0], tn), col), pl.BlockSpec((wb.shape[0], tn), col),
                  pl.BlockSpec((wc.shape[0], tn), col),
                  pl.BlockSpec((tm, tn), zcol(Z_MRG)), pl.BlockSpec((tm, tn), zcol(Z_MMLA)),
                  pl.BlockSpec((tm, tn), zcol(Z_MMEM))],
        out_specs=pl.BlockSpec((tm, tn), lambda i, j: (i, j)),
        out_shape=jax.ShapeDtypeStruct((m, n), BF),
        compiler_params=_cp("parallel", "parallel"),
        name="merge1",
    )(a, b, c, wa, wb, wc, z, z, z)


def _merge2_kernel(z_ref, w_ref, x_ref, g_ref, o_ref):
    o_ref[...] = _rms(x_ref[...] + _dot(z_ref[...], w_ref[...]), g_ref[...])


def _merge2(zz, w, x, g, tm):
    m, d = x.shape
    return pl.pallas_call(
        _merge2_kernel,
        grid=(m // tm,),
        in_specs=[pl.BlockSpec((tm, d), lambda i: (i, 0)), pl.BlockSpec(w.shape, lambda i: (0, 0)),
                  pl.BlockSpec((tm, d), lambda i: (i, 0)), pl.BlockSpec((1, d), lambda i: (0, 0))],
        out_specs=pl.BlockSpec((tm, d), lambda i: (i, 0)),
        out_shape=jax.ShapeDtypeStruct((m, d), F32),
        compiler_params=_cp("parallel"),
        name="merge2",
    )(zz, w, x, g.reshape(1, d))


def _rope_tables(pos):
    inv = ROPE_THETA ** (-jnp.arange(0, QK_ROPE, 2, dtype=F32) / QK_ROPE)
    ang = pos.astype(F32)[:, None] * inv[None, :]
    cos, sin = jnp.cos(ang), jnp.sin(ang)
    zero = jnp.zeros((pos.shape[0], LANES - QK_ROPE), F32)
    return (jnp.concatenate([cos, cos, zero], axis=1),
            jnp.concatenate([-sin, sin, zero], axis=1))


def kernel(x_prompt, x_sample, mem_prompt, cache_ckv, cache_krope, cache_mem_k, cache_mem_v, state_conv, state_rglru, page_table, norm_in, w_in, conv_w, conv_b, rg_wa, rg_ba, rg_wi, rg_bi, rg_lambda, kv_norm, w_uk, w_uv, mem_norm, w_mk, w_mv, w_rg_o, w_mla_o, w_mem_o, w_out, final_norm):
    bp, seq, d_model = x_prompt.shape
    bd, dec_seq, _ = x_sample.shape
    assert bp == 1 and dec_seq == 1
    d_rnn = conv_w.shape[1]
    n_pages = page_table.shape[1]
    past_len = n_pages * PAGE_SIZE
    half = QK_ROPE // 2

    d_q = N_HEADS * (QK_NOPE + QK_ROPE)
    d_v = N_HEADS * V_DIM
    d_m = MEM_HEADS * MEM_DIM
    o = [0]
    for sz in (d_rnn, d_rnn, d_q, KV_RANK, QK_ROPE, d_v, d_m, d_m, d_model, d_model, d_model):
        o.append(o[-1] + sz)
    assert (o[2], o[3], o[5]) == (W_Q_ROW, W_KV_ROW, W_GMLA_ROW) and o[-1] - o[5] + o[2] == Z_COLS
    wt = jnp.transpose(w_in).astype(BF)
    wuk2 = w_uk.reshape(KV_RANK, N_HEADS * QK_NOPE).astype(BF)
    wuv2 = w_uv.reshape(KV_RANK, d_v).astype(BF)
    wuk_t = jnp.transpose(w_uk, (1, 2, 0)).astype(BF)
    wuv_t = jnp.transpose(w_uv, (1, 0, 2)).astype(BF)
    w_rg_o_b, w_mla_o_b, w_mem_o_b, w_out_b = (w.astype(BF) for w in (w_rg_o, w_mla_o, w_mem_o, w_out))
    wa_b, wi_b = rg_wa.astype(BF), rg_wi.astype(BF)
    w_mkv = jnp.concatenate([w_mk, w_mv], axis=1).astype(BF)

    xp = x_prompt.reshape(seq, d_model)
    xs = x_sample.reshape(bd, d_model)

    cos_p, sin_p = _rope_tables(jnp.arange(seq))
    xn_p = _norm_cast(xp, norm_in, 512)
    z_p = _inproj(xn_p, wt, 1024, 1024, "inproj")
    q_p = _qproj(xn_p, wt, cos_p, sin_p, 2048)
    ckv_p, kr_p, ckvb_p, krb_p = _kvproj(xn_p, wt, cos_p, sin_p, kv_norm, 512)
    k_p, v_p = _kvup(ckvb_p, krb_p, wuk2, wuv2, 512)
    b_p = _flash(q_p, k_p, v_p, z_p, 512)
    a_p, conv_p, h_p = _rg_prompt(z_p, conv_w, conv_b, wa_b, wi_b, rg_ba, rg_bi, rg_lambda, 512)
    mn = _norm_cast(mem_prompt.reshape(-1, d_model), mem_norm, 256)
    mkv = _matmul(mn, w_mkv, 256, 1024, F32, "memkv")
    mem_k, mem_v = mkv[:, :d_m], mkv[:, d_m:]
    c_p = _memattn(z_p, mem_k, mem_v, 512)
    zz_p = _merge1(a_p, b_p, c_p, w_rg_o_b, w_mla_o_b, w_mem_o_b, z_p, 1024, 512)
    y_p = _merge2(zz_p, w_out_b, xp, final_norm, 512)

    cos_s, sin_s = _rope_tables(jnp.full((bd,), past_len))
    xn_s = _norm_cast(xs, norm_in, bd)
    z_s = _inproj(xn_s, wt, bd, 1024, "inproj_s")
    q_s = _qproj(xn_s, wt, cos_s, sin_s, bd)
    ckv_s, kr_s, ckvb_s, krb_s = _kvproj(xn_s, wt, cos_s, sin_s, kv_norm, bd)
    qcat = jnp.transpose(_qlat(q_s, wuk_t), (1, 0, 2))
    lat = _decode(page_table, qcat, ckvb_s, krb_s, cache_ckv, jnp.transpose(cache_krope, (0, 2, 1)), 32)
    b_s = _uvproj(lat, wuv_t, z_s)
    a_s, conv_s, h_s = _rg_sample(z_s, state_conv, state_rglru, conv_w, conv_b, wa_b, wi_b,
                                  rg_ba, rg_bi, rg_lambda)
    heads = lambda off: z_s[:, off:off + d_m].reshape(bd, MEM_HEADS, MEM_DIM)
    c_s = _memattn_s(heads(Z_QMEM), heads(Z_GMEM), cache_mem_k, cache_mem_v, 8).reshape(bd, d_m)
    zz_s = _merge1(a_s, b_s, c_s, w_rg_o_b, w_mla_o_b, w_mem_o_b, z_s, bd, 512)
    y_s = _merge2(zz_s, w_out_b, xs, final_norm, bd)

    n_mem = mem_prompt.shape[1]
    return (y_p.reshape(1, seq, d_model), y_s.reshape(bd, 1, d_model),
            ckv_p.reshape(1, seq, KV_RANK), kr_p.reshape(1, seq, QK_ROPE),
            conv_p.reshape(1, CONV_W - 1, d_rnn), h_p.reshape(1, d_rnn),
            mem_k.reshape(1, n_mem, MEM_HEADS, MEM_DIM), mem_v.reshape(1, n_mem, MEM_HEADS, MEM_DIM),
            ckv_s.reshape(bd, 1, KV_RANK), kr_s.reshape(bd, 1, QK_ROPE),
            conv_s.reshape(bd, CONV_W - 1, d_rnn), h_s)
```

```python
import functools
import math

import jax
import jax.numpy as jnp
from jax import lax
from jax.experimental import pallas as pl
from jax.experimental.pallas import tpu as pltpu

F32 = jnp.float32
BF = jnp.bfloat16

EPS = 1e-6
RG_BLOCKS = 8
RG_BLOCK = 128
CONV_W = 4
RG_C = 8.0
N_HEADS = 16
QK_NOPE = 128
QK_ROPE = 64
V_DIM = 128
KV_RANK = 512
ROPE_THETA = 10000.0
MEM_HEADS = 4
MEM_DIM = 256
PAGE_SIZE = 128
LANES = 128
HEAD_PAD = 256
LAT_PAD = KV_RANK + LANES
VMEM_LIMIT = 48 * 1024 * 1024
QK_SCALE2 = (QK_NOPE + QK_ROPE) ** -0.5 * math.log2(math.e)

Z_XR, Z_GRG, Z_GMLA, Z_QMEM, Z_GMEM, Z_MRG, Z_MMLA, Z_MMEM, Z_COLS = (
    0, 1024, 2048, 4096, 5120, 6144, 8192, 10240, 12288)
W_Q_ROW, W_KV_ROW, W_GMLA_ROW = 2048, 5120, 5696
ROW_ALIGN = 64


def _cp(*sem):
    return pltpu.CompilerParams(dimension_semantics=sem, vmem_limit_bytes=VMEM_LIMIT)


def _silu(g):
    return g * jax.nn.sigmoid(g)


def _rms(x, g):
    return x * lax.rsqrt(jnp.mean(x * x, axis=-1, keepdims=True) + EPS) * g


def _dot(a, b):
    return jnp.dot(a, b, preferred_element_type=F32)


def _dot_t(a, b):
    return lax.dot_general(a, b, (((1,), (1,)), ((), ())), preferred_element_type=F32)


def _norm_cast_kernel(x_ref, g_ref, o_ref):
    o_ref[...] = _rms(x_ref[...], g_ref[...]).astype(o_ref.dtype)


def _norm_cast(x, g, tm):
    m, d = x.shape
    return pl.pallas_call(
        _norm_cast_kernel,
        grid=(m // tm,),
        in_specs=[pl.BlockSpec((tm, d), lambda i: (i, 0)),
                  pl.BlockSpec((1, d), lambda i: (0, 0))],
        out_specs=pl.BlockSpec((tm, d), lambda i: (i, 0)),
        out_shape=jax.ShapeDtypeStruct((m, d), BF),
        compiler_params=_cp("parallel"),
        name="norm_cast",
    )(x, g.reshape(1, d))


def _mm_kernel(a_ref, w_ref, o_ref):
    o_ref[...] = _dot(a_ref[...], w_ref[...]).astype(o_ref.dtype)


def _matmul(a, w, tm, tn, out_dtype, name):
    m, k = a.shape
    n = w.shape[1]
    return pl.pallas_call(
        _mm_kernel,
        grid=(m // tm, n // tn),
        in_specs=[pl.BlockSpec((tm, k), lambda i, j: (i, 0)),
                  pl.BlockSpec((k, tn), lambda i, j: (0, j))],
        out_specs=pl.BlockSpec((tm, tn), lambda i, j: (i, j)),
        out_shape=jax.ShapeDtypeStruct((m, n), out_dtype),
        compiler_params=_cp("parallel", "parallel"),
        name=name,
    )(a, w)


def _inproj_kernel(a_ref, w_ref, o_ref, wb_sc):
    @pl.when(pl.program_id(1) == 0)
    def _():
        wb_sc[...] = w_ref[...].astype(BF)

    o_ref[...] = _dot_t(a_ref[...], wb_sc[...])


def _inproj(xn, wt, tm, tn, name):
    m, k = xn.shape
    assert W_Q_ROW % tn == 0
    skipped = W_GMLA_ROW - W_Q_ROW
    assert skipped % ROW_ALIGN == 0 and tn % ROW_ALIGN == 0
    wrow = lambda j, i: (pl.multiple_of(j * tn + jnp.where(j * tn < W_Q_ROW, 0, skipped), ROW_ALIGN), 0)
    return pl.pallas_call(
        _inproj_kernel,
        grid=(Z_COLS // tn, m // tm),
        in_specs=[pl.BlockSpec((tm, k), lambda j, i: (i, 0)),
                  pl.BlockSpec((pl.Element(tn), pl.Element(k)), wrow)],
        out_specs=pl.BlockSpec((tm, tn), lambda j, i: (i, j)),
        out_shape=jax.ShapeDtypeStruct((m, Z_COLS), F32),
        scratch_shapes=[pltpu.VMEM((tn, k), BF)],
        compiler_params=_cp("parallel", "arbitrary"),
        name=name,
    )(xn, wt)


def _rope_hi(hi, c, s):
    half = QK_ROPE // 2
    lane = lax.broadcasted_iota(jnp.int32, hi.shape, 1)
    swapped = jnp.where(lane < half, pltpu.roll(hi, LANES - half, 1), pltpu.roll(hi, half, 1))
    return hi * c + swapped * s


def _qproj_kernel(a_ref, w_ref, c_ref, s_ref, o_ref):
    res = _dot_t(a_ref[...], w_ref[...].astype(BF)) * QK_SCALE2
    o_ref[0, :, :QK_NOPE] = res[:, :QK_NOPE].astype(BF)
    o_ref[0, :, QK_NOPE:] = _rope_hi(res[:, QK_NOPE:], c_ref[...], s_ref[...]).astype(BF)


def _qproj(xn, wt, cos_t, sin_t, tm):
    m, d = xn.shape
    return pl.pallas_call(
        _qproj_kernel,
        grid=(m // tm, N_HEADS),
        in_specs=[pl.BlockSpec((tm, d), lambda i, h: (i, 0)),
                  pl.BlockSpec((pl.Element(HEAD_PAD), pl.Element(d)),
                               lambda i, h: (pl.multiple_of(W_Q_ROW + h * (QK_NOPE + QK_ROPE), ROW_ALIGN), 0)),
                  pl.BlockSpec((tm, LANES), lambda i, h: (i, 0)),
                  pl.BlockSpec((tm, LANES), lambda i, h: (i, 0))],
        out_specs=pl.BlockSpec((1, tm, HEAD_PAD), lambda i, h: (h, i, 0)),
        out_shape=jax.ShapeDtypeStruct((N_HEADS, m, HEAD_PAD), BF),
        compiler_params=_cp("parallel", "parallel"),
        name="qproj",
    )(xn, wt, cos_t, sin_t)


def _kvproj_kernel(a_ref, w_ref, c_ref, s_ref, g_ref, ckv_ref, kr_ref, ckvb_ref, krb_ref, wb_sc):
    @pl.when(pl.program_id(0) == 0)
    def _():
        wb_sc[...] = w_ref[...].astype(BF)

    res = _dot_t(a_ref[...], wb_sc[...])
    ckv = _rms(res[:, :KV_RANK], g_ref[...])
    ckv_ref[...] = ckv
    ckvb_ref[...] = ckv.astype(BF)
    rot = _rope_hi(res[:, KV_RANK:], c_ref[...], s_ref[...])
    kr_ref[...] = rot[:, :QK_ROPE]
    krb_ref[...] = rot.astype(BF)


def _kvproj(xn, wt, cos_t, sin_t, kv_norm, tm):
    m, d = xn.shape
    row = lambda i: (i, 0)
    fix = lambda i: (0, 0)
    return pl.pallas_call(
        _kvproj_kernel,
        grid=(m // tm,),
        in_specs=[pl.BlockSpec((tm, d), row),
                  pl.BlockSpec((pl.Element(KV_RANK + LANES), pl.Element(d)), lambda i: (W_KV_ROW, 0)),
                  pl.BlockSpec((tm, LANES), row), pl.BlockSpec((tm, LANES), row),
                  pl.BlockSpec((1, KV_RANK), fix)],
        out_specs=[pl.BlockSpec((tm, KV_RANK), row), pl.BlockSpec((tm, QK_ROPE), row),
                   pl.BlockSpec((tm, KV_RANK), row), pl.BlockSpec((tm, LANES), row)],
        out_shape=[jax.ShapeDtypeStruct((m, KV_RANK), F32), jax.ShapeDtypeStruct((m, QK_ROPE), F32),
                   jax.ShapeDtypeStruct((m, KV_RANK), BF), jax.ShapeDtypeStruct((m, LANES), BF)],
        scratch_shapes=[pltpu.VMEM((KV_RANK + LANES, d), BF)],
        compiler_params=_cp("arbitrary"),
        name="kvproj",
    )(xn, wt, cos_t, sin_t, kv_norm.reshape(1, KV_RANK))


def _kvup_kernel(c_ref, kr_ref, wk_ref, wv_ref, k_ref, v_ref):
    c = c_ref[...]
    kn = _dot(c, wk_ref[...])
    v = _dot(c, wv_ref[...])
    kr = kr_ref[...]
    ones = jnp.ones((c.shape[0], V_DIM), BF)
    for h in range(N_HEADS):
        k_ref[h, :, :QK_NOPE] = kn[:, h * QK_NOPE:(h + 1) * QK_NOPE].astype(BF)
        k_ref[h, :, QK_NOPE:] = kr
        v_ref[h, :, :V_DIM] = v[:, h * V_DIM:(h + 1) * V_DIM].astype(BF)
        v_ref[h, :, V_DIM:] = ones


def _kvup(ckvb, krb, wuk, wuv, tm):
    s = ckvb.shape[0]
    row = lambda i: (i, 0)
    fix = lambda i: (0, 0)
    return pl.pallas_call(
        _kvup_kernel,
        grid=(s // tm,),
        in_specs=[pl.BlockSpec((tm, KV_RANK), row), pl.BlockSpec((tm, LANES), row),
                  pl.BlockSpec(wuk.shape, fix), pl.BlockSpec(wuv.shape, fix)],
        out_specs=[pl.BlockSpec((N_HEADS, tm, HEAD_PAD), lambda i: (0, i, 0)),
                   pl.BlockSpec((N_HEADS, tm, 2 * V_DIM), lambda i: (0, i, 0))],
        out_shape=[jax.ShapeDtypeStruct((N_HEADS, s, HEAD_PAD), BF),
                   jax.ShapeDtypeStruct((N_HEADS, s, 2 * V_DIM), BF)],
        compiler_params=_cp("parallel"),
        name="kvup",
    )(ckvb, krb, wuk, wuv)


def _flash_kernel(q_ref, k_ref, v_ref, g_ref, o_ref, sa, sb, mxa, mxb, m_sc, acc_sc, *, tq):
    qi = pl.program_id(1)
    q = q_ref[0]
    m_sc[...] = jnp.full(m_sc.shape, -jnp.inf, F32)
    acc_sc[...] = jnp.zeros(acc_sc.shape, F32)

    def stage_x(t, s_buf, mx_buf):
        start = pl.multiple_of(t * tq, tq)
        s = _dot_t(q, k_ref[0, pl.ds(start, tq), :])
        s_buf[...] = s
        mx_buf[...] = jnp.broadcast_to(jnp.max(s, axis=1, keepdims=True), mx_buf.shape)

    def stage_y(t, s_buf, mx_buf, masked):
        s = s_buf[...]
        if masked:
            row = lax.broadcasted_iota(jnp.int32, (tq, tq), 0)
            col = lax.broadcasted_iota(jnp.int32, (tq, tq), 1)
            s = jnp.where(col <= row, s, -jnp.inf)
            mx = jnp.max(s, axis=1, keepdims=True)
        else:
            mx = mx_buf[...]
        m_prev = m_sc[...]
        m_new = jnp.maximum(m_prev, mx)
        m_sc[...] = m_new
        p = jnp.exp2(s - jnp.tile(m_new, (1, tq // LANES))).astype(BF)
        alpha = jnp.exp2(m_prev - m_new)
        start = pl.multiple_of(t * tq, tq)
        acc_sc[...] = jnp.tile(alpha, (1, 2)) * acc_sc[...] + _dot(p, v_ref[0, pl.ds(start, tq), :])

    stage_x(0, sa, mxa)

    def pair(t):
        stage_x(t + 1, sb, mxb)
        stage_y(t, sa, mxa, False)
        stage_x(t + 2, sa, mxa)
        stage_y(t + 1, sb, mxb, False)

    def quad(i, carry):
        pair(4 * i)
        pair(4 * i + 2)
        return carry

    lax.fori_loop(0, qi // 4, quad, 0)

    @pl.when(qi % 4 >= 2)
    def _():
        pair(4 * (qi // 4))

    @pl.when(qi % 2 == 1)
    def _():
        stage_x(qi, sb, mxb)
        stage_y(qi - 1, sa, mxa, False)
        stage_y(qi, sb, mxb, True)

    @pl.when(qi % 2 == 0)
    def _():
        stage_y(qi, sa, mxa, True)

    o = acc_sc[:, :V_DIM] / acc_sc[:, V_DIM:]
    o_ref[...] = (o * _silu(g_ref[...])).astype(o_ref.dtype)


def _flash(q, k, v, z, tq):
    h, s, _ = q.shape
    gcol = Z_GMLA // V_DIM
    return pl.pallas_call(
        functools.partial(_flash_kernel, tq=tq),
        grid=(h, s // tq),
        in_specs=[pl.BlockSpec((1, tq, HEAD_PAD), lambda hh, i: (hh, i, 0)),
                  pl.BlockSpec((1, s, HEAD_PAD), lambda hh, i: (hh, 0, 0)),
                  pl.BlockSpec((1, s, 2 * V_DIM), lambda hh, i: (hh, 0, 0)),
                  pl.BlockSpec((tq, V_DIM), lambda hh, i: (i, gcol + hh))],
        out_specs=pl.BlockSpec((tq, V_DIM), lambda hh, i: (i, hh)),
        out_shape=jax.ShapeDtypeStruct((s, h * V_DIM), BF),
        scratch_shapes=[pltpu.VMEM((tq, tq), F32), pltpu.VMEM((tq, tq), F32),
                        pltpu.VMEM((tq, LANES), F32), pltpu.VMEM((tq, LANES), F32),
                        pltpu.VMEM((tq, LANES), F32), pltpu.VMEM((tq, 2 * V_DIM), F32)],
        compiler_params=_cp("parallel", "arbitrary"),
        name="flash",
    )(q, k, v, z)


def _decode_kernel(pt_ref, q_ref, cn_ref, kn_ref, ckv_hbm, kr_hbm, o_ref, cbuf, rbuf, sem, *, npg, nchunk):
    b = pl.program_id(0)
    nb = pl.num_programs(0)

    def copies(seq, j, slot, wait_only=False):
        cps = []
        for p in range(npg):
            page = 0 if wait_only else pt_ref[(seq * nchunk + j) * npg + p]
            keys = pl.ds(p * PAGE_SIZE, PAGE_SIZE)
            cps.append(pltpu.make_async_copy(ckv_hbm.at[page], cbuf.at[slot, keys, :], sem.at[0, slot]))
            cps.append(pltpu.make_async_copy(kr_hbm.at[page], rbuf.at[slot, :, keys], sem.at[1, slot]))
        return cps

    @pl.when(b == 0)
    def _():
        for cp in copies(0, 0, 0):
            cp.start()

    q = q_ref[0]
    ql = q[:, :KV_RANK]
    qr = q[:, KV_RANK:KV_RANK + QK_ROPE]
    m = jnp.full((N_HEADS, 1), -jnp.inf, F32)
    l = jnp.zeros((N_HEADS, 1), F32)
    acc = jnp.zeros((N_HEADS, KV_RANK), F32)
    for j in range(nchunk):
        slot = j % 2
        if j + 1 < nchunk:
            for cp in copies(b, j + 1, 1 - slot):
                cp.start()
        else:
            @pl.when(b + 1 < nb)
            def _():
                for cp in copies(b + 1, 0, 1 - slot):
                    cp.start()
        for cp in copies(b, j, slot, wait_only=True):
            cp.wait()
        cb = cbuf[slot].astype(BF)
        s = _dot_t(ql, cb) + _dot(qr, rbuf[slot].astype(BF))
        m_new = jnp.maximum(m, jnp.max(s, axis=1, keepdims=True))
        alpha = jnp.exp2(m - m_new)
        p = jnp.exp2(s - m_new)
        l = alpha * l + jnp.sum(p, axis=1, keepdims=True)
        acc = alpha * acc + _dot(p.astype(BF), cb)
        m = m_new

    qf = q.astype(F32)
    cn = cn_ref[0].astype(F32)
    kn = kn_ref[0].astype(F32)
    s_new = (jnp.sum(qf[:, :KV_RANK] * cn, axis=1, keepdims=True)
             + jnp.sum(qf[:, KV_RANK:] * kn, axis=1, keepdims=True))
    m_fin = jnp.maximum(m, s_new)
    a_fin = jnp.exp2(m - m_fin)
    p_new = jnp.exp2(s_new - m_fin)
    l_fin = a_fin * l + p_new
    out = (a_fin * acc + p_new.astype(BF).astype(F32) * cn) / l_fin
    for h in range(N_HEADS):
        o_ref[0, :, h * KV_RANK:(h + 1) * KV_RANK] = out[h:h + 1, :]


def _decode(page_table, qcat, ckvb, krb, cache_ckv, cache_krope_t, npg):
    b, n_pages = page_table.shape
    nchunk = n_pages // npg
    assert nchunk * npg == n_pages and nchunk % 2 == 0
    per_b = lambda bb, pt: (bb, 0, 0)
    grid_spec = pltpu.PrefetchScalarGridSpec(
        num_scalar_prefetch=1,
        grid=(b,),
        in_specs=[pl.BlockSpec((1, N_HEADS, LAT_PAD), per_b),
                  pl.BlockSpec((1, 1, KV_RANK), per_b),
                  pl.BlockSpec((1, 1, LANES), per_b),
                  pl.BlockSpec(memory_space=pl.ANY),
                  pl.BlockSpec(memory_space=pl.ANY)],
        out_specs=pl.BlockSpec((1, 1, N_HEADS * KV_RANK), per_b),
        scratch_shapes=[pltpu.VMEM((2, npg * PAGE_SIZE, KV_RANK), F32),
                        pltpu.VMEM((2, QK_ROPE, npg * PAGE_SIZE), F32),
                        pltpu.SemaphoreType.DMA((2, 2))])
    out = pl.pallas_call(
        functools.partial(_decode_kernel, npg=npg, nchunk=nchunk),
        grid_spec=grid_spec,
        out_shape=jax.ShapeDtypeStruct((b, 1, N_HEADS * KV_RANK), F32),
        compiler_params=_cp("arbitrary"),
        name="decode",
    )(page_table.reshape(-1), qcat, ckvb.reshape(b, 1, KV_RANK), krb.reshape(b, 1, LANES),
      cache_ckv, cache_krope_t)
    return out.reshape(b, N_HEADS * KV_RANK)


def _qlat_kernel(q_ref, w_ref, o_ref):
    q = q_ref[0]
    o_ref[0, :, :KV_RANK] = _dot(q[:, :QK_NOPE], w_ref[0]).astype(BF)
    o_ref[0, :, KV_RANK:] = q[:, QK_NOPE:]


def _qlat(qs, wuk_t):
    h, b, _ = qs.shape
    return pl.pallas_call(
        _qlat_kernel,
        grid=(h,),
        in_specs=[pl.BlockSpec((1, b, HEAD_PAD), lambda i: (i, 0, 0)),
                  pl.BlockSpec((1, QK_NOPE, KV_RANK), lambda i: (i, 0, 0))],
        out_specs=pl.BlockSpec((1, b, LAT_PAD), lambda i: (i, 0, 0)),
        out_shape=jax.ShapeDtypeStruct((h, b, LAT_PAD), BF),
        compiler_params=_cp("parallel"),
        name="qlat",
    )(qs, wuk_t)


def _uvproj_kernel(l_ref, w_ref, g_ref, o_ref):
    o = _dot(l_ref[...].astype(BF), w_ref[0])
    o_ref[...] = (o * _silu(g_ref[...])).astype(o_ref.dtype)


def _uvproj(lat, wuv_t, z):
    b = lat.shape[0]
    gcol = Z_GMLA // V_DIM
    return pl.pallas_call(
        _uvproj_kernel,
        grid=(N_HEADS,),
        in_specs=[pl.BlockSpec((b, KV_RANK), lambda h: (0, h)),
                  pl.BlockSpec((1, KV_RANK, V_DIM), lambda h: (h, 0, 0)),
                  pl.BlockSpec((b, V_DIM), lambda h: (0, gcol + h))],
        out_specs=pl.BlockSpec((b, V_DIM), lambda h: (0, h)),
        out_shape=jax.ShapeDtypeStruct((b, N_HEADS * V_DIM), BF),
        compiler_params=_cp("parallel"),
        name="uvproj",
    )(lat, wuv_t, z)


def _rg_gates(xc, wa_ref, wi_ref, ba_ref, bi_ref, lam_ref):
    xcb = xc.astype(BF)
    blk = lambda n: slice(n * RG_BLOCK, (n + 1) * RG_BLOCK)
    ra = jnp.concatenate([_dot(xcb[:, blk(n)], wa_ref[n]) for n in range(RG_BLOCKS)], axis=1)
    ri = jnp.concatenate([_dot(xcb[:, blk(n)], wi_ref[n]) for n in range(RG_BLOCKS)], axis=1)
    r = jax.nn.sigmoid(ra + ba_ref[...])
    i = jax.nn.sigmoid(ri + bi_ref[...])
    log_a = -RG_C * r * jax.nn.softplus(-lam_ref[...])
    a = jnp.exp(log_a)
    u = jnp.sqrt(-_expm1(2.0 * log_a)) * (i * xc)
    return a, u


def _expm1(x):
    e = jnp.exp(x)
    em = e - 1.0
    ok = jnp.logical_and(em != 0.0, e > 0.0)
    kahan = em * x / jnp.log(jnp.where(ok, e, 2.0))
    return jnp.where(ok, kahan, jnp.where(em == 0.0, x, em))


def _rg_kernel(xr_ref, g_ref, cw_ref, cb_ref, wa_ref, wi_ref, ba_ref, bi_ref, lam_ref,
               y_ref, conv_ref, ht_ref, xp_sc, h_sc, a_sc, u_sc, *, tt):
    t = pl.program_id(0)
    pad = 8
    d = xr_ref.shape[1]

    @pl.when(t == 0)
    def _():
        xp_sc[0:pad, :] = jnp.zeros((pad, d), F32)
        h_sc[...] = jnp.zeros(h_sc.shape, F32)

    @pl.when(t > 0)
    def _():
        xp_sc[0:pad, :] = xp_sc[tt:tt + pad, :]

    x = xr_ref[...]
    xp_sc[pad:pad + tt, :] = x
    cw = cw_ref[...]
    xc = cb_ref[...] + cw[0:1] * xp_sc[pad - 3:pad - 3 + tt, :]
    xc = xc + cw[1:2] * xp_sc[pad - 2:pad - 2 + tt, :]
    xc = xc + cw[2:3] * xp_sc[pad - 1:pad - 1 + tt, :]
    xc = xc + cw[3:4] * x
    a, u = _rg_gates(xc, wa_ref, wi_ref, ba_ref, bi_ref, lam_ref)
    a_sc[...] = a
    u_sc[...] = u

    row = lax.broadcasted_iota(jnp.int32, (8, d), 0)

    def group(gi, h):
        s0 = pl.multiple_of(gi * 8, 8)
        aa = a_sc[pl.ds(s0, 8), :]
        uu = u_sc[pl.ds(s0, 8), :]
        for sh in (1, 2, 4):
            keep = row >= sh
            uu = jnp.where(keep, uu + aa * pltpu.roll(uu, sh, 0), uu)
            aa = jnp.where(keep, aa * pltpu.roll(aa, sh, 0), aa)
        hs = aa * h + uu
        u_sc[pl.ds(s0, 8), :] = hs
        return jnp.broadcast_to(hs[7:8, :], (8, d))

    h = lax.fori_loop(0, tt // 8, group, h_sc[...])
    h_sc[...] = h
    y_ref[...] = (u_sc[...] * _silu(g_ref[...])).astype(y_ref.dtype)

    @pl.when(t == pl.num_programs(0) - 1)
    def _():
        conv_ref[...] = xp_sc[pad + tt - (CONV_W - 1):pad + tt, :]
        ht_ref[...] = h[0:1, :]


def _rg_prompt(z, conv_w, conv_b, wa, wi, ba, bi, lam, tt):
    s = z.shape[0]
    d = conv_w.shape[1]
    fix2 = lambda i: (0, 0)
    fix3 = lambda i: (0, 0, 0)
    return pl.pallas_call(
        functools.partial(_rg_kernel, tt=tt),
        grid=(s // tt,),
        in_specs=[pl.BlockSpec((tt, d), lambda i: (i, Z_XR // d)),
                  pl.BlockSpec((tt, d), lambda i: (i, Z_GRG // d)),
                  pl.BlockSpec((CONV_W, d), fix2), pl.BlockSpec((1, d), fix2),
                  pl.BlockSpec(wa.shape, fix3), pl.BlockSpec(wi.shape, fix3),
                  pl.BlockSpec((1, d), fix2), pl.BlockSpec((1, d), fix2), pl.BlockSpec((1, d), fix2)],
        out_specs=[pl.BlockSpec((tt, d), lambda i: (i, 0)),
                   pl.BlockSpec((CONV_W - 1, d), fix2), pl.BlockSpec((1, d), fix2)],
        out_shape=[jax.ShapeDtypeStruct((s, d), BF),
                   jax.ShapeDtypeStruct((CONV_W - 1, d), F32), jax.ShapeDtypeStruct((1, d), F32)],
        scratch_shapes=[pltpu.VMEM((tt + 8, d), F32), pltpu.VMEM((8, d), F32),
                        pltpu.VMEM((tt, d), F32), pltpu.VMEM((tt, d), F32)],
        compiler_params=_cp("arbitrary"),
        name="rg_prompt",
    )(z, z, conv_w, conv_b.reshape(1, d), wa, wi, ba.reshape(1, d), bi.reshape(1, d), lam.reshape(1, d))


def _rg_sample_kernel(xr_ref, g_ref, sc_ref, h0_ref, cw_ref, cb_ref, wa_ref, wi_ref, ba_ref, bi_ref,
                      lam_ref, y_ref, conv_ref, h_ref):
    d = xr_ref.shape[1]
    x = xr_ref[...]
    cw = cw_ref[...]
    xc = cb_ref[...] + cw[0:1] * sc_ref[:, 0:d]
    xc = xc + cw[1:2] * sc_ref[:, d:2 * d]
    xc = xc + cw[2:3] * sc_ref[:, 2 * d:3 * d]
    xc = xc + cw[3:4] * x
    a, u = _rg_gates(xc, wa_ref, wi_ref, ba_ref, bi_ref, lam_ref)
    h = a * h0_ref[...] + u
    h_ref[...] = h
    y_ref[...] = (h * _silu(g_ref[...])).astype(y_ref.dtype)
    conv_ref[:, 0:d] = sc_ref[:, d:2 * d]
    conv_ref[:, d:2 * d] = sc_ref[:, 2 * d:3 * d]
    conv_ref[:, 2 * d:3 * d] = x


def _rg_sample(z, state_conv, h0, conv_w, conv_b, wa, wi, ba, bi, lam):
    b = z.shape[0]
    d = conv_w.shape[1]
    nprev = CONV_W - 1
    fix2 = lambda i: (0, 0)
    fix3 = lambda i: (0, 0, 0)
    return pl.pallas_call(
        _rg_sample_kernel,
        grid=(1,),
        in_specs=[pl.BlockSpec((b, d), lambda i: (0, Z_XR // d)),
                  pl.BlockSpec((b, d), lambda i: (0, Z_GRG // d)),
                  pl.BlockSpec((b, nprev * d), fix2), pl.BlockSpec((b, d), fix2),
                  pl.BlockSpec((CONV_W, d), fix2), pl.BlockSpec((1, d), fix2),
                  pl.BlockSpec(wa.shape, fix3), pl.BlockSpec(wi.shape, fix3),
                  pl.BlockSpec((1, d), fix2), pl.BlockSpec((1, d), fix2), pl.BlockSpec((1, d), fix2)],
        out_specs=[pl.BlockSpec((b, d), fix2), pl.BlockSpec((b, nprev * d), fix2),
                   pl.BlockSpec((b, d), fix2)],
        out_shape=[jax.ShapeDtypeStruct((b, d), BF), jax.ShapeDtypeStruct((b, nprev * d), F32),
                   jax.ShapeDtypeStruct((b, d), F32)],
        compiler_params=_cp("arbitrary"),
        name="rg_sample",
    )(z, z, state_conv.reshape(b, nprev * d), h0, conv_w, conv_b.reshape(1, d), wa, wi,
      ba.reshape(1, d), bi.reshape(1, d), lam.reshape(1, d))


def _memattn_kernel(q_ref, g_ref, k_ref, v_ref, o_ref):
    scale = MEM_DIM ** -0.5
    for h in range(MEM_HEADS):
        hs = slice(h * MEM_DIM, (h + 1) * MEM_DIM)
        s = _dot_t(q_ref[:, hs].astype(BF), k_ref[:, hs].astype(BF)) * scale
        p = jnp.exp(s - jnp.max(s, axis=1, keepdims=True))
        o = _dot(p.astype(BF), v_ref[:, hs].astype(BF)) / jnp.sum(p, axis=1, keepdims=True)
        o_ref[:, hs] = (o * _silu(g_ref[:, hs])).astype(o_ref.dtype)


def _memattn(z, k, v, tm):
    s = z.shape[0]
    d = MEM_HEADS * MEM_DIM
    nm = k.shape[0]
    return pl.pallas_call(
        _memattn_kernel,
        grid=(s // tm,),
        in_specs=[pl.BlockSpec((tm, d), lambda i: (i, Z_QMEM // d)),
                  pl.BlockSpec((tm, d), lambda i: (i, Z_GMEM // d)),
                  pl.BlockSpec((nm, d), lambda i: (0, 0)), pl.BlockSpec((nm, d), lambda i: (0, 0))],
        out_specs=pl.BlockSpec((tm, d), lambda i: (i, 0)),
        out_shape=jax.ShapeDtypeStruct((s, d), BF),
        compiler_params=_cp("parallel"),
        name="memattn",
    )(z, z, k, v)


def _memattn_s_kernel(q_ref, g_ref, k_ref, v_ref, o_ref, *, bt):
    scale = MEM_DIM ** -0.5
    for b in range(bt):
        s = jnp.sum(k_ref[b] * q_ref[b][None], axis=-1, keepdims=True) * scale
        p = jnp.exp(s - jnp.max(s, axis=0, keepdims=True))
        o = jnp.sum(p * v_ref[b], axis=0) / jnp.sum(p, axis=0)
        o_ref[b] = o * _silu(g_ref[b])


def _memattn_s(q, g, k, v, bt):
    b, nm, nh, d = k.shape
    small = pl.BlockSpec((bt, nh, d), lambda i: (i, 0, 0))
    big = pl.BlockSpec((bt, nm, nh, d), lambda i: (i, 0, 0, 0))
    return pl.pallas_call(
        functools.partial(_memattn_s_kernel, bt=bt),
        grid=(b // bt,),
        in_specs=[small, small, big, big],
        out_specs=small,
        out_shape=jax.ShapeDtypeStruct((b, nh, d), F32),
        compiler_params=_cp("parallel"),
        name="memattn_s",
    )(q, g, k, v)


def _merge1_kernel(a_ref, b_ref, c_ref, wa_ref, wb_ref, wc_ref, ma_ref, mb_ref, mc_ref, o_ref):
    sg = jax.nn.sigmoid
    o = sg(ma_ref[...]) * _dot(a_ref[...].astype(BF), wa_ref[...])
    o = o + sg(mb_ref[...]) * _dot(b_ref[...].astype(BF), wb_ref[...])
    o = o + sg(mc_ref[...]) * _dot(c_ref[...].astype(BF), wc_ref[...])
    o_ref[...] = o.astype(o_ref.dtype)


def _merge1(a, b, c, wa, wb, wc, z, tm, tn):
    m = a.shape[0]
    n = wa.shape[1]
    row = lambda i, j: (i, 0)
    col = lambda i, j: (0, j)
    zcol = lambda off: (lambda i, j: (i, off // tn + j))
    return pl.pallas_call(
        _merge1_kernel,
        grid=(m // tm, n // tn),
        in_specs=[pl.BlockSpec((tm, a.shape[1]), row), pl.BlockSpec((tm, b.shape[1]), row),
                  pl.BlockSpec((tm, c.shape[1]), row),
                  pl.BlockSpec((wa.shape[0], tn), col), pl.BlockSpec((wb.shape[0], tn), col),
                  pl.BlockSpec((wc.shape[0], tn), col),
                  pl.BlockSpec((tm, tn), zcol(Z_MRG)), pl.BlockSpec((tm, tn), zcol(Z_MMLA)),
                  pl.BlockSpec((tm, tn), zcol(Z_MMEM))],
        out_specs=pl.BlockSpec((tm, tn), lambda i, j: (i, j)),
        out_shape=jax.ShapeDtypeStruct((m, n), BF),
        compiler_params=_cp("parallel", "parallel"),
        name="merge1",
    )(a, b, c, wa, wb, wc, z, z, z)


def _merge2_kernel(z_ref, w_ref, x_ref, g_ref, o_ref):
    o_ref[...] = _rms(x_ref[...] + _dot(z_ref[...], w_ref[...]), g_ref[...])


def _merge2(zz, w, x, g, tm):
    m, d = x.shape
    return pl.pallas_call(
        _merge2_kernel,
        grid=(m // tm,),
        in_specs=[pl.BlockSpec((tm, d), lambda i: (i, 0)), pl.BlockSpec(w.shape, lambda i: (0, 0)),
                  pl.BlockSpec((tm, d), lambda i: (i, 0)), pl.BlockSpec((1, d), lambda i: (0, 0))],
        out_specs=pl.BlockSpec((tm, d), lambda i: (i, 0)),
        out_shape=jax.ShapeDtypeStruct((m, d), F32),
        compiler_params=_cp("parallel"),
        name="merge2",
    )(zz, w, x, g.reshape(1, d))


def _rope_tables(pos):
    inv = ROPE_THETA ** (-jnp.arange(0, QK_ROPE, 2, dtype=F32) / QK_ROPE)
    ang = pos.astype(F32)[:, None] * inv[None, :]
    cos, sin = jnp.cos(ang), jnp.sin(ang)
    zero = jnp.zeros((pos.shape[0], LANES - QK_ROPE), F32)
    return (jnp.concatenate([cos, cos, zero], axis=1),
            jnp.concatenate([-sin, sin, zero], axis=1))


def kernel(x_prompt, x_sample, mem_prompt, cache_ckv, cache_krope, cache_mem_k, cache_mem_v, state_conv, state_rglru, page_table, norm_in, w_in, conv_w, conv_b, rg_wa, rg_ba, rg_wi, rg_bi, rg_lambda, kv_norm, w_uk, w_uv, mem_norm, w_mk, w_mv, w_rg_o, w_mla_o, w_mem_o, w_out, final_norm):
    bp, seq, d_model = x_prompt.shape
    bd, dec_seq, _ = x_sample.shape
    assert bp == 1 and dec_seq == 1
    d_rnn = conv_w.shape[1]
    n_pages = page_table.shape[1]
    past_len = n_pages * PAGE_SIZE
    half = QK_ROPE // 2

    d_q = N_HEADS * (QK_NOPE + QK_ROPE)
    d_v = N_HEADS * V_DIM
    d_m = MEM_HEADS * MEM_DIM
    o = [0]
    for sz in (d_rnn, d_rnn, d_q, KV_RANK, QK_ROPE, d_v, d_m, d_m, d_model, d_model, d_model):
        o.append(o[-1] + sz)
    assert (o[2], o[3], o[5]) == (W_Q_ROW, W_KV_ROW, W_GMLA_ROW) and o[-1] - o[5] + o[2] == Z_COLS
    wt = jnp.transpose(w_in)
    wuk2 = w_uk.reshape(KV_RANK, N_HEADS * QK_NOPE).astype(BF)
    wuv2 = w_uv.reshape(KV_RANK, d_v).astype(BF)
    wuk_t = jnp.transpose(w_uk, (1, 2, 0)).astype(BF)
    wuv_t = jnp.transpose(w_uv, (1, 0, 2)).astype(BF)
    w_rg_o_b, w_mla_o_b, w_mem_o_b, w_out_b = (w.astype(BF) for w in (w_rg_o, w_mla_o, w_mem_o, w_out))
    wa_b, wi_b = rg_wa.astype(BF), rg_wi.astype(BF)
    w_mkv = jnp.concatenate([w_mk, w_mv], axis=1).astype(BF)

    xp = x_prompt.reshape(seq, d_model)
    xs = x_sample.reshape(bd, d_model)

    cos_p, sin_p = _rope_tables(jnp.arange(seq))
    xn_p = _norm_cast(xp, norm_in, 512)
    z_p = _inproj(xn_p, wt, 1024, 1024, "inproj")
    q_p = _qproj(xn_p, wt, cos_p, sin_p, 2048)
    ckv_p, kr_p, ckvb_p, krb_p = _kvproj(xn_p, wt, cos_p, sin_p, kv_norm, 512)
    k_p, v_p = _kvup(ckvb_p, krb_p, wuk2, wuv2, 512)
    b_p = _flash(q_p, k_p, v_p, z_p, 512)
    a_p, conv_p, h_p = _rg_prompt(z_p, conv_w, conv_b, wa_b, wi_b, rg_ba, rg_bi, rg_lambda, 512)
    mn = _norm_cast(mem_prompt.reshape(-1, d_model), mem_norm, 256)
    mkv = _matmul(mn, w_mkv, 256, 1024, F32, "memkv")
    mem_k, mem_v = mkv[:, :d_m], mkv[:, d_m:]
    c_p = _memattn(z_p, mem_k, mem_v, 512)
    zz_p = _merge1(a_p, b_p, c_p, w_rg_o_b, w_mla_o_b, w_mem_o_b, z_p, 1024, 512)
    y_p = _merge2(zz_p, w_out_b, xp, final_norm, 512)

    cos_s, sin_s = _rope_tables(jnp.full((bd,), past_len))
    xn_s = _norm_cast(xs, norm_in, bd)
    z_s = _inproj(xn_s, wt, bd, 1024, "inproj_s")
    q_s = _qproj(xn_s, wt, cos_s, sin_s, bd)
    ckv_s, kr_s, ckvb_s, krb_s = _kvproj(xn_s, wt, cos_s, sin_s, kv_norm, bd)
    qcat = jnp.transpose(_qlat(q_s, wuk_t), (1, 0, 2))
    lat = _decode(page_table, qcat, ckvb_s, krb_s, cache_ckv, jnp.transpose(cache_krope, (0, 2, 1)), 32)
    b_s = _uvproj(lat, wuv_t, z_s)
    a_s, conv_s, h_s = _rg_sample(z_s, state_conv, state_rglru, conv_w, conv_b, wa_b, wi_b,
                                  rg_ba, rg_bi, rg_lambda)
    heads = lambda off: z_s[:, off:off + d_m].reshape(bd, MEM_HEADS, MEM_DIM)
    c_s = _memattn_s(heads(Z_QMEM), heads(Z_GMEM), cache_mem_k, cache_mem_v, 8).reshape(bd, d_m)
    zz_s = _merge1(a_s, b_s, c_s, w_rg_o_b, w_mla_o_b, w_mem_o_b, z_s, bd, 512)
    y_s = _merge2(zz_s, w_out_b, xs, final_norm, bd)

    n_mem = mem_prompt.shape[1]
    return (y_p.reshape(1, seq, d_model), y_s.reshape(bd, 1, d_model),
            ckv_p.reshape(1, seq, KV_RANK), kr_p.reshape(1, seq, QK_ROPE),
            conv_p.reshape(1, CONV_W - 1, d_rnn), h_p.reshape(1, d_rnn),
            mem_k.reshape(1, n_mem, MEM_HEADS, MEM_DIM), mem_v.reshape(1, n_mem, MEM_HEADS, MEM_DIM),
            ckv_s.reshape(bd, 1, KV_RANK), kr_s.reshape(bd, 1, QK_ROPE),
            conv_s.reshape(bd, CONV_W - 1, d_rnn), h_s)
```

```python
import functools
import math

import jax
import jax.numpy as jnp
from jax import lax
from jax.experimental import pallas as pl
from jax.experimental.pallas import tpu as pltpu

F32 = jnp.float32
BF = jnp.bfloat16

EPS = 1e-6
RG_BLOCKS = 8
RG_BLOCK = 128
CONV_W = 4
RG_C = 8.0
N_HEADS = 16
QK_NOPE = 128
QK_ROPE = 64
V_DIM = 128
KV_RANK = 512
ROPE_THETA = 10000.0
MEM_HEADS = 4
MEM_DIM = 256
PAGE_SIZE = 128
LANES = 128
HEAD_PAD = 256
LAT_PAD = KV_RANK + LANES
VMEM_LIMIT = 48 * 1024 * 1024
QK_SCALE2 = (QK_NOPE + QK_ROPE) ** -0.5 * math.log2(math.e)

Z_XR, Z_GRG, Z_GMLA, Z_QMEM, Z_GMEM, Z_MRG, Z_MMLA, Z_MMEM, Z_COLS = (
    0, 1024, 2048, 4096, 5120, 6144, 8192, 10240, 12288)
W_Q_ROW, W_KV_ROW, W_GMLA_ROW = 2048, 5120, 5696
ROW_ALIGN = 64


def _cp(*sem):
    return pltpu.CompilerParams(dimension_semantics=sem, vmem_limit_bytes=VMEM_LIMIT)


def _silu(g):
    return g * jax.nn.sigmoid(g)


def _rms(x, g):
    return x * lax.rsqrt(jnp.mean(x * x, axis=-1, keepdims=True) + EPS) * g


def _dot(a, b):
    return jnp.dot(a, b, preferred_element_type=F32)


def _dot_t(a, b):
    return lax.dot_general(a, b, (((1,), (1,)), ((), ())), preferred_element_type=F32)


def _norm_cast_kernel(x_ref, g_ref, o_ref):
    o_ref[...] = _rms(x_ref[...], g_ref[...]).astype(o_ref.dtype)


def _norm_cast(x, g, tm):
    m, d = x.shape
    return pl.pallas_call(
        _norm_cast_kernel,
        grid=(m // tm,),
        in_specs=[pl.BlockSpec((tm, d), lambda i: (i, 0)),
                  pl.BlockSpec((1, d), lambda i: (0, 0))],
        out_specs=pl.BlockSpec((tm, d), lambda i: (i, 0)),
        out_shape=jax.ShapeDtypeStruct((m, d), BF),
        compiler_params=_cp("parallel"),
        name="norm_cast",
    )(x, g.reshape(1, d))


def _mm_kernel(a_ref, w_ref, o_ref):
    o_ref[...] = _dot(a_ref[...], w_ref[...]).astype(o_ref.dtype)


def _matmul(a, w, tm, tn, out_dtype, name):
    m, k = a.shape
    n = w.shape[1]
    return pl.pallas_call(
        _mm_kernel,
        grid=(m // tm, n // tn),
        in_specs=[pl.BlockSpec((tm, k), lambda i, j: (i, 0)),
                  pl.BlockSpec((k, tn), lambda i, j: (0, j))],
        out_specs=pl.BlockSpec((tm, tn), lambda i, j: (i, j)),
        out_shape=jax.ShapeDtypeStruct((m, n), out_dtype),
        compiler_params=_cp("parallel", "parallel"),
        name=name,
    )(a, w)


def _inproj_kernel(a_ref, w_ref, o_ref, wb_sc):
    @pl.when(pl.program_id(1) == 0)
    def _():
        wb_sc[...] = w_ref[...].astype(BF)

    o_ref[...] = _dot_t(a_ref[...], wb_sc[...])


def _inproj(xn, wt, tm, tn, name):
    m, k = xn.shape
    assert W_Q_ROW % tn == 0
    skipped = W_GMLA_ROW - W_Q_ROW
    assert skipped % ROW_ALIGN == 0 and tn % ROW_ALIGN == 0
    wrow = lambda j, i: (pl.multiple_of(j * tn + jnp.where(j * tn < W_Q_ROW, 0, skipped), ROW_ALIGN), 0)
    return pl.pallas_call(
        _inproj_kernel,
        grid=(Z_COLS // tn, m // tm),
        in_specs=[pl.BlockSpec((tm, k), lambda j, i: (i, 0)),
                  pl.BlockSpec((pl.Element(tn), pl.Element(k)), wrow)],
        out_specs=pl.BlockSpec((tm, tn), lambda j, i: (i, j)),
        out_shape=jax.ShapeDtypeStruct((m, Z_COLS), F32),
        scratch_shapes=[pltpu.VMEM((tn, k), BF)],
        compiler_params=_cp("parallel", "arbitrary"),
        name=name,
    )(xn, wt)


def _rope_hi(hi, c, s):
    half = QK_ROPE // 2
    lane = lax.broadcasted_iota(jnp.int32, hi.shape, 1)
    swapped = jnp.where(lane < half, pltpu.roll(hi, LANES - half, 1), pltpu.roll(hi, half, 1))
    return hi * c + swapped * s


def _qproj_kernel(a_ref, w_ref, c_ref, s_ref, o_ref):
    res = _dot_t(a_ref[...], w_ref[...].astype(BF)) * QK_SCALE2
    o_ref[0, :, :QK_NOPE] = res[:, :QK_NOPE].astype(BF)
    o_ref[0, :, QK_NOPE:] = _rope_hi(res[:, QK_NOPE:], c_ref[...], s_ref[...]).astype(BF)


def _qproj(xn, wt, cos_t, sin_t, tm):
    m, d = xn.shape
    return pl.pallas_call(
        _qproj_kernel,
        grid=(m // tm, N_HEADS),
        in_specs=[pl.BlockSpec((tm, d), lambda i, h: (i, 0)),
                  pl.BlockSpec((pl.Element(HEAD_PAD), pl.Element(d)),
                               lambda i, h: (pl.multiple_of(W_Q_ROW + h * (QK_NOPE + QK_ROPE), ROW_ALIGN), 0)),
                  pl.BlockSpec((tm, LANES), lambda i, h: (i, 0)),
                  pl.BlockSpec((tm, LANES), lambda i, h: (i, 0))],
        out_specs=pl.BlockSpec((1, tm, HEAD_PAD), lambda i, h: (h, i, 0)),
        out_shape=jax.ShapeDtypeStruct((N_HEADS, m, HEAD_PAD), BF),
        compiler_params=_cp("parallel", "parallel"),
        name="qproj",
    )(xn, wt, cos_t, sin_t)


def _kvproj_kernel(a_ref, w_ref, c_ref, s_ref, g_ref, ckv_ref, kr_ref, ckvb_ref, krb_ref, wb_sc):
    @pl.when(pl.program_id(0) == 0)
    def _():
        wb_sc[...] = w_ref[...].astype(BF)

    res = _dot_t(a_ref[...], wb_sc[...])
    ckv = _rms(res[:, :KV_RANK], g_ref[...])
    ckv_ref[...] = ckv
    ckvb_ref[...] = ckv.astype(BF)
    rot = _rope_hi(res[:, KV_RANK:], c_ref[...], s_ref[...])
    kr_ref[...] = rot[:, :QK_ROPE]
    krb_ref[...] = rot.astype(BF)


def _kvproj(xn, wt, cos_t, sin_t, kv_norm, tm):
    m, d = xn.shape
    row = lambda i: (i, 0)
    fix = lambda i: (0, 0)
    return pl.pallas_call(
        _kvproj_kernel,
        grid=(m // tm,),
        in_specs=[pl.BlockSpec((tm, d), row),
                  pl.BlockSpec((pl.Element(KV_RANK + LANES), pl.Element(d)), lambda i: (W_KV_ROW, 0)),
                  pl.BlockSpec((tm, LANES), row), pl.BlockSpec((tm, LANES), row),
                  pl.BlockSpec((1, KV_RANK), fix)],
        out_specs=[pl.BlockSpec((tm, KV_RANK), row), pl.BlockSpec((tm, QK_ROPE), row),
                   pl.BlockSpec((tm, KV_RANK), row), pl.BlockSpec((tm, LANES), row)],
        out_shape=[jax.ShapeDtypeStruct((m, KV_RANK), F32), jax.ShapeDtypeStruct((m, QK_ROPE), F32),
                   jax.ShapeDtypeStruct((m, KV_RANK), BF), jax.ShapeDtypeStruct((m, LANES), BF)],
        scratch_shapes=[pltpu.VMEM((KV_RANK + LANES, d), BF)],
        compiler_params=_cp("arbitrary"),
        name="kvproj",
    )(xn, wt, cos_t, sin_t, kv_norm.reshape(1, KV_RANK))


def _kvup_kernel(c_ref, kr_ref, wk_ref, wv_ref, k_ref, v_ref):
    c = c_ref[...]
    kn = _dot(c, wk_ref[...])
    v = _dot(c, wv_ref[...])
    kr = kr_ref[...]
    ones = jnp.ones((c.shape[0], V_DIM), BF)
    for h in range(N_HEADS):
        k_ref[h, :, :QK_NOPE] = kn[:, h * QK_NOPE:(h + 1) * QK_NOPE].astype(BF)
        k_ref[h, :, QK_NOPE:] = kr
        v_ref[h, :, :V_DIM] = v[:, h * V_DIM:(h + 1) * V_DIM].astype(BF)
        v_ref[h, :, V_DIM:] = ones


def _kvup(ckvb, krb, wuk, wuv, tm):
    s = ckvb.shape[0]
    row = lambda i: (i, 0)
    fix = lambda i: (0, 0)
    return pl.pallas_call(
        _kvup_kernel,
        grid=(s // tm,),
        in_specs=[pl.BlockSpec((tm, KV_RANK), row), pl.BlockSpec((tm, LANES), row),
                  pl.BlockSpec(wuk.shape, fix), pl.BlockSpec(wuv.shape, fix)],
        out_specs=[pl.BlockSpec((N_HEADS, tm, HEAD_PAD), lambda i: (0, i, 0)),
                   pl.BlockSpec((N_HEADS, tm, 2 * V_DIM), lambda i: (0, i, 0))],
        out_shape=[jax.ShapeDtypeStruct((N_HEADS, s, HEAD_PAD), BF),
                   jax.ShapeDtypeStruct((N_HEADS, s, 2 * V_DIM), BF)],
        compiler_params=_cp("parallel"),
        name="kvup",
    )(ckvb, krb, wuk, wuv)


def _flash_kernel(q_ref, k_ref, v_ref, g_ref, o_ref, sa, sb, mxa, mxb, m_sc, acc_sc, *, tq):
    qi = pl.program_id(1)
    q = q_ref[0]
    m_sc[...] = jnp.full(m_sc.shape, -jnp.inf, F32)
    acc_sc[...] = jnp.zeros(acc_sc.shape, F32)

    def stage_x(t, s_buf, mx_buf):
        start = pl.multiple_of(t * tq, tq)
        s = _dot_t(q, k_ref[0, pl.ds(start, tq), :])
        s_buf[...] = s
        mx_buf[...] = jnp.broadcast_to(jnp.max(s, axis=1, keepdims=True), mx_buf.shape)

    def stage_y(t, s_buf, mx_buf, masked):
        s = s_buf[...]
        if masked:
            row = lax.broadcasted_iota(jnp.int32, (tq, tq), 0)
            col = lax.broadcasted_iota(jnp.int32, (tq, tq), 1)
            s = jnp.where(col <= row, s, -jnp.inf)
            mx = jnp.max(s, axis=1, keepdims=True)
        else:
            mx = mx_buf[...]
        m_prev = m_sc[...]
        m_new = jnp.maximum(m_prev, mx)
        m_sc[...] = m_new
        p = jnp.exp2(s - jnp.tile(m_new, (1, tq // LANES))).astype(BF)
        alpha = jnp.exp2(m_prev - m_new)
        start = pl.multiple_of(t * tq, tq)
        acc_sc[...] = jnp.tile(alpha, (1, 2)) * acc_sc[...] + _dot(p, v_ref[0, pl.ds(start, tq), :])

    stage_x(0, sa, mxa)

    def pair(t):
        stage_x(t + 1, sb, mxb)
        stage_y(t, sa, mxa, False)
        stage_x(t + 2, sa, mxa)
        stage_y(t + 1, sb, mxb, False)

    def quad(i, carry):
        pair(4 * i)
        pair(4 * i + 2)
        return carry

    lax.fori_loop(0, qi // 4, quad, 0)

    @pl.when(qi % 4 >= 2)
    def _():
        pair(4 * (qi // 4))

    @pl.when(qi % 2 == 1)
    def _():
        stage_x(qi, sb, mxb)
        stage_y(qi - 1, sa, mxa, False)
        stage_y(qi, sb, mxb, True)

    @pl.when(qi % 2 == 0)
    def _():
        stage_y(qi, sa, mxa, True)

    o = acc_sc[:, :V_DIM] / acc_sc[:, V_DIM:]
    o_ref[...] = (o * _silu(g_ref[...])).astype(o_ref.dtype)


def _flash(q, k, v, z, tq):
    h, s, _ = q.shape
    gcol = Z_GMLA // V_DIM
    return pl.pallas_call(
        functools.partial(_flash_kernel, tq=tq),
        grid=(h, s // tq),
        in_specs=[pl.BlockSpec((1, tq, HEAD_PAD), lambda hh, i: (hh, i, 0)),
                  pl.BlockSpec((1, s, HEAD_PAD), lambda hh, i: (hh, 0, 0)),
                  pl.BlockSpec((1, s, 2 * V_DIM), lambda hh, i: (hh, 0, 0)),
                  pl.BlockSpec((tq, V_DIM), lambda hh, i: (i, gcol + hh))],
        out_specs=pl.BlockSpec((tq, V_DIM), lambda hh, i: (i, hh)),
        out_shape=jax.ShapeDtypeStruct((s, h * V_DIM), BF),
        scratch_shapes=[pltpu.VMEM((tq, tq), F32), pltpu.VMEM((tq, tq), F32),
                        pltpu.VMEM((tq, LANES), F32), pltpu.VMEM((tq, LANES), F32),
                        pltpu.VMEM((tq, LANES), F32), pltpu.VMEM((tq, 2 * V_DIM), F32)],
        compiler_params=_cp("parallel", "arbitrary"),
        name="flash",
    )(q, k, v, z)


def _qproj_decode_kernel(pt_ref, a_ref, w_ref, c_ref, s_ref, q_ref, cn_ref, kn_ref, ckv_hbm, kr_hbm,
                         qo_ref, o_ref, cbuf, rbuf, sem, *, npg, nchunk):
    b = pl.program_id(0) * pl.num_programs(1) + pl.program_id(1)
    nb = pl.num_programs(0) * pl.num_programs(1)
    _decode_step(b, nb, pt_ref, q_ref, cn_ref, kn_ref, ckv_hbm, kr_hbm, o_ref, cbuf, rbuf, sem, npg, nchunk)
    _qproj_kernel(a_ref, w_ref, c_ref, s_ref, qo_ref)


def _decode_step(b, nb, pt_ref, q_ref, cn_ref, kn_ref, ckv_hbm, kr_hbm, o_ref, cbuf, rbuf, sem, npg, nchunk):
    def copies(seq, j, slot, wait_only=False):
        cps = []
        for p in range(npg):
            page = 0 if wait_only else pt_ref[(seq * nchunk + j) * npg + p]
            keys = pl.ds(p * PAGE_SIZE, PAGE_SIZE)
            cps.append(pltpu.make_async_copy(ckv_hbm.at[page], cbuf.at[slot, keys, :], sem.at[0, slot]))
            cps.append(pltpu.make_async_copy(kr_hbm.at[page], rbuf.at[slot, :, keys], sem.at[1, slot]))
        return cps

    @pl.when(b == 0)
    def _():
        for cp in copies(0, 0, 0):
            cp.start()

    q = q_ref[0]
    ql = q[:, :KV_RANK]
    qr = q[:, KV_RANK:KV_RANK + QK_ROPE]
    m = jnp.full((N_HEADS, 1), -jnp.inf, F32)
    l = jnp.zeros((N_HEADS, 1), F32)
    acc = jnp.zeros((N_HEADS, KV_RANK), F32)
    for j in range(nchunk):
        slot = j % 2
        if j + 1 < nchunk:
            for cp in copies(b, j + 1, 1 - slot):
                cp.start()
        else:
            @pl.when(b + 1 < nb)
            def _():
                for cp in copies(b + 1, 0, 1 - slot):
                    cp.start()
        for cp in copies(b, j, slot, wait_only=True):
            cp.wait()
        half = npg * PAGE_SIZE // 2
        cb0 = cbuf[slot, :half, :].astype(BF)
        cb1 = cbuf[slot, half:, :].astype(BF)
        s0 = _dot_t(ql, cb0) + _dot(qr, rbuf[slot, :, :half].astype(BF))
        s1 = _dot_t(ql, cb1) + _dot(qr, rbuf[slot, :, half:].astype(BF))
        mx = jnp.maximum(jnp.max(s0, axis=1, keepdims=True), jnp.max(s1, axis=1, keepdims=True))
        m_new = jnp.maximum(m, mx)
        alpha = jnp.exp2(m - m_new)
        p0 = jnp.exp2(s0 - m_new)
        p1 = jnp.exp2(s1 - m_new)
        l = alpha * l + (jnp.sum(p0, axis=1, keepdims=True) + jnp.sum(p1, axis=1, keepdims=True))
        acc = alpha * acc + (_dot(p0.astype(BF), cb0) + _dot(p1.astype(BF), cb1))
        m = m_new

    qf = q.astype(F32)
    cn = cn_ref[0].astype(F32)
    kn = kn_ref[0].astype(F32)
    s_new = (jnp.sum(qf[:, :KV_RANK] * cn, axis=1, keepdims=True)
             + jnp.sum(qf[:, KV_RANK:] * kn, axis=1, keepdims=True))
    m_fin = jnp.maximum(m, s_new)
    a_fin = jnp.exp2(m - m_fin)
    p_new = jnp.exp2(s_new - m_fin)
    l_fin = a_fin * l + p_new
    out = (a_fin * acc + p_new.astype(BF).astype(F32) * cn) / l_fin
    for h in range(N_HEADS):
        o_ref[0, :, h * KV_RANK:(h + 1) * KV_RANK] = out[h:h + 1, :]


def _qproj_decode(xn, wt, cos_t, sin_t, page_table, qcat, ckvb, krb, cache_ckv, cache_krope_t, npg):
    m, d = xn.shape
    b, n_pages = page_table.shape
    nchunk = n_pages // npg
    assert nchunk * npg == n_pages and nchunk % 2 == 0
    assert (m * N_HEADS) % b == 0
    tm = m * N_HEADS // b
    row = lambda i, h, pt: (i, 0)
    per_b = lambda i, h, pt: (i * N_HEADS + h, 0, 0)
    wrow = lambda i, h, pt: (pl.multiple_of(W_Q_ROW + h * (QK_NOPE + QK_ROPE), ROW_ALIGN), 0)
    grid_spec = pltpu.PrefetchScalarGridSpec(
        num_scalar_prefetch=1,
        grid=(m // tm, N_HEADS),
        in_specs=[pl.BlockSpec((tm, d), row),
                  pl.BlockSpec((pl.Element(HEAD_PAD), pl.Element(d)), wrow),
                  pl.BlockSpec((tm, LANES), row), pl.BlockSpec((tm, LANES), row),
                  pl.BlockSpec((1, N_HEADS, LAT_PAD), per_b),
                  pl.BlockSpec((1, 1, KV_RANK), per_b),
                  pl.BlockSpec((1, 1, LANES), per_b),
                  pl.BlockSpec(memory_space=pl.ANY),
                  pl.BlockSpec(memory_space=pl.ANY)],
        out_specs=[pl.BlockSpec((1, tm, HEAD_PAD), lambda i, h, pt: (h, i, 0)),
                   pl.BlockSpec((1, 1, N_HEADS * KV_RANK), per_b)],
        scratch_shapes=[pltpu.VMEM((2, npg * PAGE_SIZE, KV_RANK), F32),
                        pltpu.VMEM((2, QK_ROPE, npg * PAGE_SIZE), F32),
                        pltpu.SemaphoreType.DMA((2, 2))])
    q, lat = pl.pallas_call(
        functools.partial(_qproj_decode_kernel, npg=npg, nchunk=nchunk),
        grid_spec=grid_spec,
        out_shape=[jax.ShapeDtypeStruct((N_HEADS, m, HEAD_PAD), BF),
                   jax.ShapeDtypeStruct((b, 1, N_HEADS * KV_RANK), F32)],
        compiler_params=_cp("arbitrary", "arbitrary"),
        name="qproj_decode",
    )(page_table.reshape(-1), xn, wt, cos_t, sin_t, qcat, ckvb.reshape(b, 1, KV_RANK),
      krb.reshape(b, 1, LANES), cache_ckv, cache_krope_t)
    return q, lat.reshape(b, N_HEADS * KV_RANK)


def _qlat_kernel(q_ref, w_ref, o_ref):
    q = q_ref[0]
    o_ref[0, :, :KV_RANK] = _dot(q[:, :QK_NOPE], w_ref[0]).astype(BF)
    o_ref[0, :, KV_RANK:] = q[:, QK_NOPE:]


def _qlat(qs, wuk_t):
    h, b, _ = qs.shape
    return pl.pallas_call(
        _qlat_kernel,
        grid=(h,),
        in_specs=[pl.BlockSpec((1, b, HEAD_PAD), lambda i: (i, 0, 0)),
                  pl.BlockSpec((1, QK_NOPE, KV_RANK), lambda i: (i, 0, 0))],
        out_specs=pl.BlockSpec((1, b, LAT_PAD), lambda i: (i, 0, 0)),
        out_shape=jax.ShapeDtypeStruct((h, b, LAT_PAD), BF),
        compiler_params=_cp("parallel"),
        name="qlat",
    )(qs, wuk_t)


def _uvproj_kernel(l_ref, w_ref, g_ref, o_ref):
    o = _dot(l_ref[...].astype(BF), w_ref[0])
    o_ref[...] = (o * _silu(g_ref[...])).astype(o_ref.dtype)


def _uvproj(lat, wuv_t, z):
    b = lat.shape[0]
    gcol = Z_GMLA // V_DIM
    return pl.pallas_call(
        _uvproj_kernel,
        grid=(N_HEADS,),
        in_specs=[pl.BlockSpec((b, KV_RANK), lambda h: (0, h)),
                  pl.BlockSpec((1, KV_RANK, V_DIM), lambda h: (h, 0, 0)),
                  pl.BlockSpec((b, V_DIM), lambda h: (0, gcol + h))],
        out_specs=pl.BlockSpec((b, V_DIM), lambda h: (0, h)),
        out_shape=jax.ShapeDtypeStruct((b, N_HEADS * V_DIM), BF),
        compiler_params=_cp("parallel"),
        name="uvproj",
    )(lat, wuv_t, z)


def _rg_gates(xc, wa_ref, wi_ref, ba_ref, bi_ref, lam_ref):
    xcb = xc.astype(BF)
    blk = lambda n: slice(n * RG_BLOCK, (n + 1) * RG_BLOCK)
    ra = jnp.concatenate([_dot(xcb[:, blk(n)], wa_ref[n]) for n in range(RG_BLOCKS)], axis=1)
    ri = jnp.concatenate([_dot(xcb[:, blk(n)], wi_ref[n]) for n in range(RG_BLOCKS)], axis=1)
    r = jax.nn.sigmoid(ra + ba_ref[...])
    i = jax.nn.sigmoid(ri + bi_ref[...])
    log_a = -RG_C * r * jax.nn.softplus(-lam_ref[...])
    a = jnp.exp(log_a)
    u = jnp.sqrt(-_expm1(2.0 * log_a)) * (i * xc)
    return a, u


def _expm1(x):
    e = jnp.exp(x)
    em = e - 1.0
    ok = jnp.logical_and(em != 0.0, e > 0.0)
    kahan = em * x / jnp.log(jnp.where(ok, e, 2.0))
    return jnp.where(ok, kahan, jnp.where(em == 0.0, x, em))


def _rg_kernel(xr_ref, g_ref, cw_ref, cb_ref, wa_ref, wi_ref, ba_ref, bi_ref, lam_ref,
               y_ref, conv_ref, ht_ref, xp_sc, h_sc, a_sc, u_sc, *, tt):
    t = pl.program_id(0)
    pad = 8
    d = xr_ref.shape[1]

    @pl.when(t == 0)
    def _():
        xp_sc[0:pad, :] = jnp.zeros((pad, d), F32)
        h_sc[...] = jnp.zeros(h_sc.shape, F32)

    @pl.when(t > 0)
    def _():
        xp_sc[0:pad, :] = xp_sc[tt:tt + pad, :]

    x = xr_ref[...]
    xp_sc[pad:pad + tt, :] = x
    cw = cw_ref[...]
    xc = cb_ref[...] + cw[0:1] * xp_sc[pad - 3:pad - 3 + tt, :]
    xc = xc + cw[1:2] * xp_sc[pad - 2:pad - 2 + tt, :]
    xc = xc + cw[2:3] * xp_sc[pad - 1:pad - 1 + tt, :]
    xc = xc + cw[3:4] * x
    a, u = _rg_gates(xc, wa_ref, wi_ref, ba_ref, bi_ref, lam_ref)
    a_sc[...] = a
    u_sc[...] = u

    row = lax.broadcasted_iota(jnp.int32, (8, d), 0)

    def group(gi, h):
        s0 = pl.multiple_of(gi * 8, 8)
        aa = a_sc[pl.ds(s0, 8), :]
        uu = u_sc[pl.ds(s0, 8), :]
        for sh in (1, 2, 4):
            keep = row >= sh
            uu = jnp.where(keep, uu + aa * pltpu.roll(uu, sh, 0), uu)
            aa = jnp.where(keep, aa * pltpu.roll(aa, sh, 0), aa)
        hs = aa * h + uu
        u_sc[pl.ds(s0, 8), :] = hs
        return jnp.broadcast_to(hs[7:8, :], (8, d))

    h = lax.fori_loop(0, tt // 8, group, h_sc[...])
    h_sc[...] = h
    y_ref[...] = (u_sc[...] * _silu(g_ref[...])).astype(y_ref.dtype)

    @pl.when(t == pl.num_programs(0) - 1)
    def _():
        conv_ref[...] = xp_sc[pad + tt - (CONV_W - 1):pad + tt, :]
        ht_ref[...] = h[0:1, :]


def _rg_prompt(z, conv_w, conv_b, wa, wi, ba, bi, lam, tt):
    s = z.shape[0]
    d = conv_w.shape[1]
    fix2 = lambda i: (0, 0)
    fix3 = lambda i: (0, 0, 0)
    return pl.pallas_call(
        functools.partial(_rg_kernel, tt=tt),
        grid=(s // tt,),
        in_specs=[pl.BlockSpec((tt, d), lambda i: (i, Z_XR // d)),
                  pl.BlockSpec((tt, d), lambda i: (i, Z_GRG // d)),
                  pl.BlockSpec((CONV_W, d), fix2), pl.BlockSpec((1, d), fix2),
                  pl.BlockSpec(wa.shape, fix3), pl.BlockSpec(wi.shape, fix3),
                  pl.BlockSpec((1, d), fix2), pl.BlockSpec((1, d), fix2), pl.BlockSpec((1, d), fix2)],
        out_specs=[pl.BlockSpec((tt, d), lambda i: (i, 0)),
                   pl.BlockSpec((CONV_W - 1, d), fix2), pl.BlockSpec((1, d), fix2)],
        out_shape=[jax.ShapeDtypeStruct((s, d), BF),
                   jax.ShapeDtypeStruct((CONV_W - 1, d), F32), jax.ShapeDtypeStruct((1, d), F32)],
        scratch_shapes=[pltpu.VMEM((tt + 8, d), F32), pltpu.VMEM((8, d), F32),
                        pltpu.VMEM((tt, d), F32), pltpu.VMEM((tt, d), F32)],
        compiler_params=_cp("arbitrary"),
        name="rg_prompt",
    )(z, z, conv_w, conv_b.reshape(1, d), wa, wi, ba.reshape(1, d), bi.reshape(1, d), lam.reshape(1, d))


def _rg_sample_kernel(xr_ref, g_ref, sc_ref, h0_ref, cw_ref, cb_ref, wa_ref, wi_ref, ba_ref, bi_ref,
                      lam_ref, y_ref, conv_ref, h_ref):
    d = xr_ref.shape[1]
    x = xr_ref[...]
    cw = cw_ref[...]
    xc = cb_ref[...] + cw[0:1] * sc_ref[:, 0:d]
    xc = xc + cw[1:2] * sc_ref[:, d:2 * d]
    xc = xc + cw[2:3] * sc_ref[:, 2 * d:3 * d]
    xc = xc + cw[3:4] * x
    a, u = _rg_gates(xc, wa_ref, wi_ref, ba_ref, bi_ref, lam_ref)
    h = a * h0_ref[...] + u
    h_ref[...] = h
    y_ref[...] = (h * _silu(g_ref[...])).astype(y_ref.dtype)
    conv_ref[:, 0:d] = sc_ref[:, d:2 * d]
    conv_ref[:, d:2 * d] = sc_ref[:, 2 * d:3 * d]
    conv_ref[:, 2 * d:3 * d] = x


def _rg_sample(z, state_conv, h0, conv_w, conv_b, wa, wi, ba, bi, lam):
    b = z.shape[0]
    d = conv_w.shape[1]
    nprev = CONV_W - 1
    fix2 = lambda i: (0, 0)
    fix3 = lambda i: (0, 0, 0)
    return pl.pallas_call(
        _rg_sample_kernel,
        grid=(1,),
        in_specs=[pl.BlockSpec((b, d), lambda i: (0, Z_XR // d)),
                  pl.BlockSpec((b, d), lambda i: (0, Z_GRG // d)),
                  pl.BlockSpec((b, nprev * d), fix2), pl.BlockSpec((b, d), fix2),
                  pl.BlockSpec((CONV_W, d), fix2), pl.BlockSpec((1, d), fix2),
                  pl.BlockSpec(wa.shape, fix3), pl.BlockSpec(wi.shape, fix3),
                  pl.BlockSpec((1, d), fix2), pl.BlockSpec((1, d), fix2), pl.BlockSpec((1, d), fix2)],
        out_specs=[pl.BlockSpec((b, d), fix2), pl.BlockSpec((b, nprev * d), fix2),
                   pl.BlockSpec((b, d), fix2)],
        out_shape=[jax.ShapeDtypeStruct((b, d), BF), jax.ShapeDtypeStruct((b, nprev * d), F32),
                   jax.ShapeDtypeStruct((b, d), F32)],
        compiler_params=_cp("arbitrary"),
        name="rg_sample",
    )(z, z, state_conv.reshape(b, nprev * d), h0, conv_w, conv_b.reshape(1, d), wa, wi,
      ba.reshape(1, d), bi.reshape(1, d), lam.reshape(1, d))


def _memattn_kernel(q_ref, g_ref, k_ref, v_ref, o_ref):
    scale = MEM_DIM ** -0.5
    for h in range(MEM_HEADS):
        hs = slice(h * MEM_DIM, (h + 1) * MEM_DIM)
        s = _dot_t(q_ref[:, hs].astype(BF), k_ref[:, hs].astype(BF)) * scale
        p = jnp.exp(s - jnp.max(s, axis=1, keepdims=True))
        o = _dot(p.astype(BF), v_ref[:, hs].astype(BF)) / jnp.sum(p, axis=1, keepdims=True)
        o_ref[:, hs] = (o * _silu(g_ref[:, hs])).astype(o_ref.dtype)


def _memattn(z, k, v, tm):
    s = z.shape[0]
    d = MEM_HEADS * MEM_DIM
    nm = k.shape[0]
    return pl.pallas_call(
        _memattn_kernel,
        grid=(s // tm,),
        in_specs=[pl.BlockSpec((tm, d), lambda i: (i, Z_QMEM // d)),
                  pl.BlockSpec((tm, d), lambda i: (i, Z_GMEM // d)),
                  pl.BlockSpec((nm, d), lambda i: (0, 0)), pl.BlockSpec((nm, d), lambda i: (0, 0))],
        out_specs=pl.BlockSpec((tm, d), lambda i: (i, 0)),
        out_shape=jax.ShapeDtypeStruct((s, d), BF),
        compiler_params=_cp("parallel"),
        name="memattn",
    )(z, z, k, v)


def _memattn_s_kernel(q_ref, g_ref, k_ref, v_ref, o_ref, *, bt):
    scale = MEM_DIM ** -0.5
    for b in range(bt):
        s = jnp.sum(k_ref[b] * q_ref[b][None], axis=-1, keepdims=True) * scale
        p = jnp.exp(s - jnp.max(s, axis=0, keepdims=True))
        o = jnp.sum(p * v_ref[b], axis=0) / jnp.sum(p, axis=0)
        o_ref[b] = o * _silu(g_ref[b])


def _memattn_s(q, g, k, v, bt):
    b, nm, nh, d = k.shape
    small = pl.BlockSpec((bt, nh, d), lambda i: (i, 0, 0))
    big = pl.BlockSpec((bt, nm, nh, d), lambda i: (i, 0, 0, 0))
    return pl.pallas_call(
        functools.partial(_memattn_s_kernel, bt=bt),
        grid=(b // bt,),
        in_specs=[small, small, big, big],
        out_specs=small,
        out_shape=jax.ShapeDtypeStruct((b, nh, d), F32),
        compiler_params=_cp("parallel"),
        name="memattn_s",
    )(q, g, k, v)


def _merge1_kernel(a_ref, b_ref, c_ref, wa_ref, wb_ref, wc_ref, ma_ref, mb_ref, mc_ref, o_ref):
    sg = jax.nn.sigmoid
    o = sg(ma_ref[...]) * _dot(a_ref[...].astype(BF), wa_ref[...])
    o = o + sg(mb_ref[...]) * _dot(b_ref[...].astype(BF), wb_ref[...])
    o = o + sg(mc_ref[...]) * _dot(c_ref[...].astype(BF), wc_ref[...])
    o_ref[...] = o.astype(o_ref.dtype)


def _merge1(a, b, c, wa, wb, wc, z, tm, tn):
    m = a.shape[0]
    n = wa.shape[1]
    row = lambda i, j: (i, 0)
    col = lambda i, j: (0, j)
    zcol = lambda off: (lambda i, j: (i, off // tn + j))
    return pl.pallas_call(
        _merge1_kernel,
        grid=(m // tm, n // tn),
        in_specs=[pl.BlockSpec((tm, a.shape[1]), row), pl.BlockSpec((tm, b.shape[1]), row),
                  pl.BlockSpec((tm, c.shape[1]), row),
                  pl.BlockSpec((wa.shape[0], tn), col), pl.BlockSpec((wb.shape[0], tn), col),
                  pl.BlockSpec((wc.shape[0], tn), col),
                  pl.BlockSpec((tm, tn), zcol(Z_MRG)), pl.BlockSpec((tm, tn), zcol(Z_MMLA)),
                  pl.BlockSpec((tm, tn), zcol(Z_MMEM))],
        out_specs=pl.BlockSpec((tm, tn), lambda i, j: (i, j)),
        out_shape=jax.ShapeDtypeStruct((m, n), BF),
        compiler_params=_cp("parallel", "parallel"),
        name="merge1",
    )(a, b, c, wa, wb, wc, z, z, z)


def _merge2_kernel(z_ref, w_ref, x_ref, g_ref, o_ref):
    o_ref[...] = _rms(x_ref[...] + _dot(z_ref[...], w_ref[...]), g_ref[...])


def _merge2(zz, w, x, g, tm):
    m, d = x.shape
    return pl.pallas_call(
        _merge2_kernel,
        grid=(m // tm,),
        in_specs=[pl.BlockSpec((tm, d), lambda i: (i, 0)), pl.BlockSpec(w.shape, lambda i: (0, 0)),
                  pl.BlockSpec((tm, d), lambda i: (i, 0)), pl.BlockSpec((1, d), lambda i: (0, 0))],
        out_specs=pl.BlockSpec((tm, d), lambda i: (i, 0)),
        out_shape=jax.ShapeDtypeStruct((m, d), F32),
        compiler_params=_cp("parallel"),
        name="merge2",
    )(zz, w, x, g.reshape(1, d))


def _rope_tables(pos):
    inv = ROPE_THETA ** (-jnp.arange(0, QK_ROPE, 2, dtype=F32) / QK_ROPE)
    ang = pos.astype(F32)[:, None] * inv[None, :]
    cos, sin = jnp.cos(ang), jnp.sin(ang)
    zero = jnp.zeros((pos.shape[0], LANES - QK_ROPE), F32)
    return (jnp.concatenate([cos, cos, zero], axis=1),
            jnp.concatenate([-sin, sin, zero], axis=1))


def kernel(x_prompt, x_sample, mem_prompt, cache_ckv, cache_krope, cache_mem_k, cache_mem_v, state_conv, state_rglru, page_table, norm_in, w_in, conv_w, conv_b, rg_wa, rg_ba, rg_wi, rg_bi, rg_lambda, kv_norm, w_uk, w_uv, mem_norm, w_mk, w_mv, w_rg_o, w_mla_o, w_mem_o, w_out, final_norm):
    bp, seq, d_model = x_prompt.shape
    bd, dec_seq, _ = x_sample.shape
    assert bp == 1 and dec_seq == 1
    d_rnn = conv_w.shape[1]
    n_pages = page_table.shape[1]
    past_len = n_pages * PAGE_SIZE
    half = QK_ROPE // 2

    d_q = N_HEADS * (QK_NOPE + QK_ROPE)
    d_v = N_HEADS * V_DIM
    d_m = MEM_HEADS * MEM_DIM
    o = [0]
    for sz in (d_rnn, d_rnn, d_q, KV_RANK, QK_ROPE, d_v, d_m, d_m, d_model, d_model, d_model):
        o.append(o[-1] + sz)
    assert (o[2], o[3], o[5]) == (W_Q_ROW, W_KV_ROW, W_GMLA_ROW) and o[-1] - o[5] + o[2] == Z_COLS
    wt = jnp.transpose(w_in)
    wuk2 = w_uk.reshape(KV_RANK, N_HEADS * QK_NOPE).astype(BF)
    wuv2 = w_uv.reshape(KV_RANK, d_v).astype(BF)
    wuk_t = jnp.transpose(w_uk, (1, 2, 0)).astype(BF)
    wuv_t = jnp.transpose(w_uv, (1, 0, 2)).astype(BF)
    w_rg_o_b, w_mla_o_b, w_mem_o_b, w_out_b = (w.astype(BF) for w in (w_rg_o, w_mla_o, w_mem_o, w_out))
    wa_b, wi_b = rg_wa.astype(BF), rg_wi.astype(BF)
    w_mkv = jnp.concatenate([w_mk, w_mv], axis=1).astype(BF)

    xp = x_prompt.reshape(seq, d_model)
    xs = x_sample.reshape(bd, d_model)

    cos_s, sin_s = _rope_tables(jnp.full((bd,), past_len))
    xn_s = _norm_cast(xs, norm_in, bd)
    z_s = _inproj(xn_s, wt, bd, 1024, "inproj_s")
    q_s = _qproj(xn_s, wt, cos_s, sin_s, bd)
    ckv_s, kr_s, ckvb_s, krb_s = _kvproj(xn_s, wt, cos_s, sin_s, kv_norm, bd)
    qcat = jnp.transpose(_qlat(q_s, wuk_t), (1, 0, 2))

    cos_p, sin_p = _rope_tables(jnp.arange(seq))
    xn_p = _norm_cast(xp, norm_in, 512)
    q_p, lat = _qproj_decode(xn_p, wt, cos_p, sin_p, page_table, qcat, ckvb_s, krb_s, cache_ckv,
                             jnp.transpose(cache_krope, (0, 2, 1)), 32)
    z_p = _inproj(xn_p, wt, 1024, 1024, "inproj")
    ckv_p, kr_p, ckvb_p, krb_p = _kvproj(xn_p, wt, cos_p, sin_p, kv_norm, 512)
    k_p, v_p = _kvup(ckvb_p, krb_p, wuk2, wuv2, 512)
    b_p = _flash(q_p, k_p, v_p, z_p, 512)
    a_p, conv_p, h_p = _rg_prompt(z_p, conv_w, conv_b, wa_b, wi_b, rg_ba, rg_bi, rg_lambda, 512)
    mn = _norm_cast(mem_prompt.reshape(-1, d_model), mem_norm, 256)
    mkv = _matmul(mn, w_mkv, 256, 1024, F32, "memkv")
    mem_k, mem_v = mkv[:, :d_m], mkv[:, d_m:]
    c_p = _memattn(z_p, mem_k, mem_v, 512)
    zz_p = _merge1(a_p, b_p, c_p, w_rg_o_b, w_mla_o_b, w_mem_o_b, z_p, 1024, 512)
    y_p = _merge2(zz_p, w_out_b, xp, final_norm, 512)

    b_s = _uvproj(lat, wuv_t, z_s)
    a_s, conv_s, h_s = _rg_sample(z_s, state_conv, state_rglru, conv_w, conv_b, wa_b, wi_b,
                                  rg_ba, rg_bi, rg_lambda)
    heads = lambda off: z_s[:, off:off + d_m].reshape(bd, MEM_HEADS, MEM_DIM)
    c_s = _memattn_s(heads(Z_QMEM), heads(Z_GMEM), cache_mem_k, cache_mem_v, 8).reshape(bd, d_m)
    zz_s = _merge1(a_s, b_s, c_s, w_rg_o_b, w_mla_o_b, w_mem_o_b, z_s, bd, 512)
    y_s = _merge2(zz_s, w_out_b, xs, final_norm, bd)

    n_mem = mem_prompt.shape[1]
    return (y_p.reshape(1, seq, d_model), y_s.reshape(bd, 1, d_model),
            ckv_p.reshape(1, seq, KV_RANK), kr_p.reshape(1, seq, QK_ROPE),
            conv_p.reshape(1, CONV_W - 1, d_rnn), h_p.reshape(1, d_rnn),
            mem_k.reshape(1, n_mem, MEM_HEADS, MEM_DIM), mem_v.reshape(1, n_mem, MEM_HEADS, MEM_DIM),
            ckv_s.reshape(bd, 1, KV_RANK), kr_s.reshape(bd, 1, QK_ROPE),
            conv_s.reshape(bd, CONV_W - 1, d_rnn), h_s)
```

```python
import functools
import math

import jax
import jax.numpy as jnp
from jax import lax
from jax.experimental import pallas as pl
from jax.experimental.pallas import tpu as pltpu

F32 = jnp.float32
BF = jnp.bfloat16

EPS = 1e-6
RG_BLOCKS = 8
RG_BLOCK = 128
CONV_W = 4
RG_C = 8.0
N_HEADS = 16
QK_NOPE = 128
QK_ROPE = 64
V_DIM = 128
KV_RANK = 512
ROPE_THETA = 10000.0
MEM_HEADS = 4
MEM_DIM = 256
PAGE_SIZE = 128
LANES = 128
HEAD_PAD = 256
LAT_PAD = KV_RANK + LANES
VMEM_LIMIT = 48 * 1024 * 1024
DECODE_VMEM_LIMIT = 56 * 1024 * 1024
QK_SCALE2 = (QK_NOPE + QK_ROPE) ** -0.5 * math.log2(math.e)

Z_XR, Z_GRG, Z_GMLA, Z_QMEM, Z_GMEM, Z_MRG, Z_MMLA, Z_MMEM, Z_COLS = (
    0, 1024, 2048, 4096, 5120, 6144, 8192, 10240, 12288)
W_Q_ROW, W_KV_ROW, W_GMLA_ROW = 2048, 5120, 5696
ROW_ALIGN = 64
DECODE_SLOTS = 4


def _cp(*sem):
    return pltpu.CompilerParams(dimension_semantics=sem, vmem_limit_bytes=VMEM_LIMIT)


def _silu(g):
    return g * jax.nn.sigmoid(g)


def _rms(x, g):
    return x * lax.rsqrt(jnp.mean(x * x, axis=-1, keepdims=True) + EPS) * g


def _dot(a, b):
    return jnp.dot(a, b, preferred_element_type=F32)


def _dot_t(a, b):
    return lax.dot_general(a, b, (((1,), (1,)), ((), ())), preferred_element_type=F32)


def _norm_cast_kernel(x_ref, g_ref, o_ref):
    o_ref[...] = _rms(x_ref[...], g_ref[...]).astype(o_ref.dtype)


def _norm_cast(x, g, tm):
    m, d = x.shape
    return pl.pallas_call(
        _norm_cast_kernel,
        grid=(m // tm,),
        in_specs=[pl.BlockSpec((tm, d), lambda i: (i, 0)),
                  pl.BlockSpec((1, d), lambda i: (0, 0))],
        out_specs=pl.BlockSpec((tm, d), lambda i: (i, 0)),
        out_shape=jax.ShapeDtypeStruct((m, d), BF),
        compiler_params=_cp("parallel"),
        name="norm_cast",
    )(x, g.reshape(1, d))


def _mm_kernel(a_ref, w_ref, o_ref):
    o_ref[...] = _dot(a_ref[...], w_ref[...]).astype(o_ref.dtype)


def _matmul(a, w, tm, tn, out_dtype, name):
    m, k = a.shape
    n = w.shape[1]
    return pl.pallas_call(
        _mm_kernel,
        grid=(m // tm, n // tn),
        in_specs=[pl.BlockSpec((tm, k), lambda i, j: (i, 0)),
                  pl.BlockSpec((k, tn), lambda i, j: (0, j))],
        out_specs=pl.BlockSpec((tm, tn), lambda i, j: (i, j)),
        out_shape=jax.ShapeDtypeStruct((m, n), out_dtype),
        compiler_params=_cp("parallel", "parallel"),
        name=name,
    )(a, w)


def _inproj_kernel(a_ref, w_ref, o_ref, wb_sc):
    @pl.when(pl.program_id(1) == 0)
    def _():
        wb_sc[...] = w_ref[...].astype(BF)

    o_ref[...] = _dot_t(a_ref[...], wb_sc[...])


def _inproj(xn, wt, tm, tn, name):
    m, k = xn.shape
    assert W_Q_ROW % tn == 0
    skipped = W_GMLA_ROW - W_Q_ROW
    assert skipped % ROW_ALIGN == 0 and tn % ROW_ALIGN == 0
    wrow = lambda j, i: (pl.multiple_of(j * tn + jnp.where(j * tn < W_Q_ROW, 0, skipped), ROW_ALIGN), 0)
    return pl.pallas_call(
        _inproj_kernel,
        grid=(Z_COLS // tn, m // tm),
        in_specs=[pl.BlockSpec((tm, k), lambda j, i: (i, 0)),
                  pl.BlockSpec((pl.Element(tn), pl.Element(k)), wrow)],
        out_specs=pl.BlockSpec((tm, tn), lambda j, i: (i, j)),
        out_shape=jax.ShapeDtypeStruct((m, Z_COLS), F32),
        scratch_shapes=[pltpu.VMEM((tn, k), BF)],
        compiler_params=_cp("parallel", "arbitrary"),
        name=name,
    )(xn, wt)


def _rope_hi(hi, c, s):
    half = QK_ROPE // 2
    lane = lax.broadcasted_iota(jnp.int32, hi.shape, 1)
    swapped = jnp.where(lane < half, pltpu.roll(hi, LANES - half, 1), pltpu.roll(hi, half, 1))
    return hi * c + swapped * s


def _qproj_kernel(a_ref, w_ref, c_ref, s_ref, o_ref):
    res = _dot_t(a_ref[...], w_ref[...].astype(BF)) * QK_SCALE2
    o_ref[0, :, :QK_NOPE] = res[:, :QK_NOPE].astype(BF)
    o_ref[0, :, QK_NOPE:] = _rope_hi(res[:, QK_NOPE:], c_ref[...], s_ref[...]).astype(BF)


def _qproj(xn, wt, cos_t, sin_t, tm):
    m, d = xn.shape
    return pl.pallas_call(
        _qproj_kernel,
        grid=(m // tm, N_HEADS),
        in_specs=[pl.BlockSpec((tm, d), lambda i, h: (i, 0)),
                  pl.BlockSpec((pl.Element(HEAD_PAD), pl.Element(d)),
                               lambda i, h: (pl.multiple_of(W_Q_ROW + h * (QK_NOPE + QK_ROPE), ROW_ALIGN), 0)),
                  pl.BlockSpec((tm, LANES), lambda i, h: (i, 0)),
                  pl.BlockSpec((tm, LANES), lambda i, h: (i, 0))],
        out_specs=pl.BlockSpec((1, tm, HEAD_PAD), lambda i, h: (h, i, 0)),
        out_shape=jax.ShapeDtypeStruct((N_HEADS, m, HEAD_PAD), BF),
        compiler_params=_cp("parallel", "parallel"),
        name="qproj",
    )(xn, wt, cos_t, sin_t)


def _kvproj_kernel(a_ref, w_ref, c_ref, s_ref, g_ref, ckv_ref, kr_ref, ckvb_ref, krb_ref, wb_sc):
    @pl.when(pl.program_id(0) == 0)
    def _():
        wb_sc[...] = w_ref[...].astype(BF)

    res = _dot_t(a_ref[...], wb_sc[...])
    ckv = _rms(res[:, :KV_RANK], g_ref[...])
    ckv_ref[...] = ckv
    ckvb_ref[...] = ckv.astype(BF)
    rot = _rope_hi(res[:, KV_RANK:], c_ref[...], s_ref[...])
    kr_ref[...] = rot[:, :QK_ROPE]
    krb_ref[...] = rot.astype(BF)


def _kvproj(xn, wt, cos_t, sin_t, kv_norm, tm):
    m, d = xn.shape
    row = lambda i: (i, 0)
    fix = lambda i: (0, 0)
    return pl.pallas_call(
        _kvproj_kernel,
        grid=(m // tm,),
        in_specs=[pl.BlockSpec((tm, d), row),
                  pl.BlockSpec((pl.Element(KV_RANK + LANES), pl.Element(d)), lambda i: (W_KV_ROW, 0)),
                  pl.BlockSpec((tm, LANES), row), pl.BlockSpec((tm, LANES), row),
                  pl.BlockSpec((1, KV_RANK), fix)],
        out_specs=[pl.BlockSpec((tm, KV_RANK), row), pl.BlockSpec((tm, QK_ROPE), row),
                   pl.BlockSpec((tm, KV_RANK), row), pl.BlockSpec((tm, LANES), row)],
        out_shape=[jax.ShapeDtypeStruct((m, KV_RANK), F32), jax.ShapeDtypeStruct((m, QK_ROPE), F32),
                   jax.ShapeDtypeStruct((m, KV_RANK), BF), jax.ShapeDtypeStruct((m, LANES), BF)],
        scratch_shapes=[pltpu.VMEM((KV_RANK + LANES, d), BF)],
        compiler_params=_cp("arbitrary"),
        name="kvproj",
    )(xn, wt, cos_t, sin_t, kv_norm.reshape(1, KV_RANK))


def _kvup_kernel(c_ref, kr_ref, wk_ref, wv_ref, k_ref, v_ref):
    c = c_ref[...]
    kn = _dot(c, wk_ref[...])
    v = _dot(c, wv_ref[...])
    kr = kr_ref[...]
    ones = jnp.ones((c.shape[0], V_DIM), BF)
    for h in range(N_HEADS):
        k_ref[h, :, :QK_NOPE] = kn[:, h * QK_NOPE:(h + 1) * QK_NOPE].astype(BF)
        k_ref[h, :, QK_NOPE:] = kr
        v_ref[h, :, :V_DIM] = v[:, h * V_DIM:(h + 1) * V_DIM].astype(BF)
        v_ref[h, :, V_DIM:] = ones


def _kvup(ckvb, krb, wuk, wuv, tm):
    s = ckvb.shape[0]
    row = lambda i: (i, 0)
    fix = lambda i: (0, 0)
    return pl.pallas_call(
        _kvup_kernel,
        grid=(s // tm,),
        in_specs=[pl.BlockSpec((tm, KV_RANK), row), pl.BlockSpec((tm, LANES), row),
                  pl.BlockSpec(wuk.shape, fix), pl.BlockSpec(wuv.shape, fix)],
        out_specs=[pl.BlockSpec((N_HEADS, tm, HEAD_PAD), lambda i: (0, i, 0)),
                   pl.BlockSpec((N_HEADS, tm, 2 * V_DIM), lambda i: (0, i, 0))],
        out_shape=[jax.ShapeDtypeStruct((N_HEADS, s, HEAD_PAD), BF),
                   jax.ShapeDtypeStruct((N_HEADS, s, 2 * V_DIM), BF)],
        compiler_params=_cp("parallel"),
        name="kvup",
    )(ckvb, krb, wuk, wuv)


def _flash_kernel(q_ref, k_ref, v_ref, g_ref, o_ref, sa, sb, mxa, mxb, m_sc, acc_sc, *, tq):
    qi = pl.program_id(1)
    q = q_ref[0]
    m_sc[...] = jnp.full(m_sc.shape, -jnp.inf, F32)
    acc_sc[...] = jnp.zeros(acc_sc.shape, F32)

    def stage_x(t, s_buf, mx_buf):
        start = pl.multiple_of(t * tq, tq)
        s = _dot_t(q, k_ref[0, pl.ds(start, tq), :])
        s_buf[...] = s
        mx_buf[...] = jnp.broadcast_to(jnp.max(s, axis=1, keepdims=True), mx_buf.shape)

    def stage_y(t, s_buf, mx_buf, masked):
        s = s_buf[...]
        if masked:
            row = lax.broadcasted_iota(jnp.int32, (tq, tq), 0)
            col = lax.broadcasted_iota(jnp.int32, (tq, tq), 1)
            s = jnp.where(col <= row, s, -jnp.inf)
            mx = jnp.max(s, axis=1, keepdims=True)
        else:
            mx = mx_buf[...]
        m_prev = m_sc[...]
        m_new = jnp.maximum(m_prev, mx)
        m_sc[...] = m_new
        p = jnp.exp2(s - jnp.tile(m_new, (1, tq // LANES))).astype(BF)
        alpha = jnp.exp2(m_prev - m_new)
        start = pl.multiple_of(t * tq, tq)
        acc_sc[...] = jnp.tile(alpha, (1, 2)) * acc_sc[...] + _dot(p, v_ref[0, pl.ds(start, tq), :])

    stage_x(0, sa, mxa)

    def pair(t):
        stage_x(t + 1, sb, mxb)
        stage_y(t, sa, mxa, False)
        stage_x(t + 2, sa, mxa)
        stage_y(t + 1, sb, mxb, False)

    def quad(i, carry):
        pair(4 * i)
        pair(4 * i + 2)
        return carry

    lax.fori_loop(0, qi // 4, quad, 0)

    @pl.when(qi % 4 >= 2)
    def _():
        pair(4 * (qi // 4))

    @pl.when(qi % 2 == 1)
    def _():
        stage_x(qi, sb, mxb)
        stage_y(qi - 1, sa, mxa, False)
        stage_y(qi, sb, mxb, True)

    @pl.when(qi % 2 == 0)
    def _():
        stage_y(qi, sa, mxa, True)

    o = acc_sc[:, :V_DIM] / acc_sc[:, V_DIM:]
    o_ref[...] = (o * _silu(g_ref[...])).astype(o_ref.dtype)


def _flash(q, k, v, z, tq):
    h, s, _ = q.shape
    gcol = Z_GMLA // V_DIM
    return pl.pallas_call(
        functools.partial(_flash_kernel, tq=tq),
        grid=(h, s // tq),
        in_specs=[pl.BlockSpec((1, tq, HEAD_PAD), lambda hh, i: (hh, i, 0)),
                  pl.BlockSpec((1, s, HEAD_PAD), lambda hh, i: (hh, 0, 0)),
                  pl.BlockSpec((1, s, 2 * V_DIM), lambda hh, i: (hh, 0, 0)),
                  pl.BlockSpec((tq, V_DIM), lambda hh, i: (i, gcol + hh))],
        out_specs=pl.BlockSpec((tq, V_DIM), lambda hh, i: (i, hh)),
        out_shape=jax.ShapeDtypeStruct((s, h * V_DIM), BF),
        scratch_shapes=[pltpu.VMEM((tq, tq), F32), pltpu.VMEM((tq, tq), F32),
                        pltpu.VMEM((tq, LANES), F32), pltpu.VMEM((tq, LANES), F32),
                        pltpu.VMEM((tq, LANES), F32), pltpu.VMEM((tq, 2 * V_DIM), F32)],
        compiler_params=_cp("parallel", "arbitrary"),
        name="flash",
    )(q, k, v, z)


def _qproj_decode_kernel(pt_ref, a_ref, w_ref, c_ref, s_ref, q_ref, cn_ref, kn_ref, ckv_hbm, kr_hbm,
                         qo_ref, o_ref, cbuf, rbuf, sem, *, npg, nchunk):
    b = pl.program_id(0) * pl.num_programs(1) + pl.program_id(1)
    nb = pl.num_programs(0) * pl.num_programs(1)
    _decode_step(b, nb, pt_ref, q_ref, cn_ref, kn_ref, ckv_hbm, kr_hbm, o_ref, cbuf, rbuf, sem, npg, nchunk)
    _qproj_kernel(a_ref, w_ref, c_ref, s_ref, qo_ref)


def _decode_step(b, nb, pt_ref, q_ref, cn_ref, kn_ref, ckv_hbm, kr_hbm, o_ref, cbuf, rbuf, sem, npg, nchunk):
    nslot = cbuf.shape[0]
    ahead = nslot - 1
    def copies(seq, j, slot, wait_only=False):
        cps = []
        for p in range(npg):
            page = 0 if wait_only else pt_ref[(seq * nchunk + j) * npg + p]
            keys = pl.ds(p * PAGE_SIZE, PAGE_SIZE)
            cps.append(pltpu.make_async_copy(ckv_hbm.at[page], cbuf.at[slot, keys, :], sem.at[0, slot]))
            cps.append(pltpu.make_async_copy(kr_hbm.at[page], rbuf.at[slot, :, keys], sem.at[1, slot]))
        return cps

    @pl.when(b == 0)
    def _():
        for j in range(ahead):
            for cp in copies(0, j, j):
                cp.start()

    q = q_ref[0]
    ql = q[:, :KV_RANK]
    qr = q[:, KV_RANK:KV_RANK + QK_ROPE]
    m = jnp.full((N_HEADS, 1), -jnp.inf, F32)
    l = jnp.zeros((N_HEADS, 1), F32)
    acc = jnp.zeros((N_HEADS, KV_RANK), F32)
    for j in range(nchunk):
        slot = j % nslot
        jn = j + ahead
        if jn < nchunk:
            for cp in copies(b, jn, jn % nslot):
                cp.start()
        else:
            @pl.when(b + 1 < nb)
            def _(jn=jn):
                for cp in copies(b + 1, jn - nchunk, (jn - nchunk) % nslot):
                    cp.start()
        for cp in copies(b, j, slot, wait_only=True):
            cp.wait()
        half = npg * PAGE_SIZE // 2
        cb0 = cbuf[slot, :half, :].astype(BF)
        cb1 = cbuf[slot, half:, :].astype(BF)
        s0 = _dot_t(ql, cb0) + _dot(qr, rbuf[slot, :, :half].astype(BF))
        s1 = _dot_t(ql, cb1) + _dot(qr, rbuf[slot, :, half:].astype(BF))
        mx = jnp.maximum(jnp.max(s0, axis=1, keepdims=True), jnp.max(s1, axis=1, keepdims=True))
        m_new = jnp.maximum(m, mx)
        alpha = jnp.exp2(m - m_new)
        p0 = jnp.exp2(s0 - m_new)
        p1 = jnp.exp2(s1 - m_new)
        l = alpha * l + (jnp.sum(p0, axis=1, keepdims=True) + jnp.sum(p1, axis=1, keepdims=True))
        acc = alpha * acc + (_dot(p0.astype(BF), cb0) + _dot(p1.astype(BF), cb1))
        m = m_new

    qf = q.astype(F32)
    cn = cn_ref[0].astype(F32)
    kn = kn_ref[0].astype(F32)
    s_new = (jnp.sum(qf[:, :KV_RANK] * cn, axis=1, keepdims=True)
             + jnp.sum(qf[:, KV_RANK:] * kn, axis=1, keepdims=True))
    m_fin = jnp.maximum(m, s_new)
    a_fin = jnp.exp2(m - m_fin)
    p_new = jnp.exp2(s_new - m_fin)
    l_fin = a_fin * l + p_new
    out = (a_fin * acc + p_new.astype(BF).astype(F32) * cn) / l_fin
    for h in range(N_HEADS):
        o_ref[0, :, h * KV_RANK:(h + 1) * KV_RANK] = out[h:h + 1, :]


def _qproj_decode(xn, wt, cos_t, sin_t, page_table, qcat, ckvb, krb, cache_ckv, cache_krope_t, npg):
    m, d = xn.shape
    b, n_pages = page_table.shape
    nchunk = n_pages // npg
    assert nchunk * npg == n_pages and nchunk % DECODE_SLOTS == 0 and nchunk >= DECODE_SLOTS
    assert (m * N_HEADS) % b == 0
    tm = m * N_HEADS // b
    row = lambda i, h, pt: (i, 0)
    per_b = lambda i, h, pt: (i * N_HEADS + h, 0, 0)
    wrow = lambda i, h, pt: (pl.multiple_of(W_Q_ROW + h * (QK_NOPE + QK_ROPE), ROW_ALIGN), 0)
    grid_spec = pltpu.PrefetchScalarGridSpec(
        num_scalar_prefetch=1,
        grid=(m // tm, N_HEADS),
        in_specs=[pl.BlockSpec((tm, d), row),
                  pl.BlockSpec((pl.Element(HEAD_PAD), pl.Element(d)), wrow),
                  pl.BlockSpec((tm, LANES), row), pl.BlockSpec((tm, LANES), row),
                  pl.BlockSpec((1, N_HEADS, LAT_PAD), per_b),
                  pl.BlockSpec((1, 1, KV_RANK), per_b),
                  pl.BlockSpec((1, 1, LANES), per_b),
                  pl.BlockSpec(memory_space=pl.ANY),
                  pl.BlockSpec(memory_space=pl.ANY)],
        out_specs=[pl.BlockSpec((1, tm, HEAD_PAD), lambda i, h, pt: (h, i, 0)),
                   pl.BlockSpec((1, 1, N_HEADS * KV_RANK), per_b)],
        scratch_shapes=[pltpu.VMEM((DECODE_SLOTS, npg * PAGE_SIZE, KV_RANK), F32),
                        pltpu.VMEM((DECODE_SLOTS, QK_ROPE, npg * PAGE_SIZE), F32),
                        pltpu.SemaphoreType.DMA((2, DECODE_SLOTS))])
    q, lat = pl.pallas_call(
        functools.partial(_qproj_decode_kernel, npg=npg, nchunk=nchunk),
        grid_spec=grid_spec,
        out_shape=[jax.ShapeDtypeStruct((N_HEADS, m, HEAD_PAD), BF),
                   jax.ShapeDtypeStruct((b, 1, N_HEADS * KV_RANK), F32)],
        compiler_params=pltpu.CompilerParams(dimension_semantics=("arbitrary", "arbitrary"),
                                             vmem_limit_bytes=DECODE_VMEM_LIMIT),
        name="qproj_decode",
    )(page_table.reshape(-1), xn, wt, cos_t, sin_t, qcat, ckvb.reshape(b, 1, KV_RANK),
      krb.reshape(b, 1, LANES), cache_ckv, cache_krope_t)
    return q, lat.reshape(b, N_HEADS * KV_RANK)


def _qlat_kernel(q_ref, w_ref, o_ref):
    q = q_ref[0]
    o_ref[0, :, :KV_RANK] = _dot(q[:, :QK_NOPE], w_ref[0]).astype(BF)
    o_ref[0, :, KV_RANK:] = q[:, QK_NOPE:]


def _qlat(qs, wuk_t):
    h, b, _ = qs.shape
    return pl.pallas_call(
        _qlat_kernel,
        grid=(h,),
        in_specs=[pl.BlockSpec((1, b, HEAD_PAD), lambda i: (i, 0, 0)),
                  pl.BlockSpec((1, QK_NOPE, KV_RANK), lambda i: (i, 0, 0))],
        out_specs=pl.BlockSpec((1, b, LAT_PAD), lambda i: (i, 0, 0)),
        out_shape=jax.ShapeDtypeStruct((h, b, LAT_PAD), BF),
        compiler_params=_cp("parallel"),
        name="qlat",
    )(qs, wuk_t)


def _uvproj_kernel(l_ref, w_ref, g_ref, o_ref):
    o = _dot(l_ref[...].astype(BF), w_ref[0])
    o_ref[...] = (o * _silu(g_ref[...])).astype(o_ref.dtype)


def _uvproj(lat, wuv_t, z):
    b = lat.shape[0]
    gcol = Z_GMLA // V_DIM
    return pl.pallas_call(
        _uvproj_kernel,
        grid=(N_HEADS,),
        in_specs=[pl.BlockSpec((b, KV_RANK), lambda h: (0, h)),
                  pl.BlockSpec((1, KV_RANK, V_DIM), lambda h: (h, 0, 0)),
                  pl.BlockSpec((b, V_DIM), lambda h: (0, gcol + h))],
        out_specs=pl.BlockSpec((b, V_DIM), lambda h: (0, h)),
        out_shape=jax.ShapeDtypeStruct((b, N_HEADS * V_DIM), BF),
        compiler_params=_cp("parallel"),
        name="uvproj",
    )(lat, wuv_t, z)


def _rg_gates(xc, wa_ref, wi_ref, ba_ref, bi_ref, lam_ref):
    xcb = xc.astype(BF)
    blk = lambda n: slice(n * RG_BLOCK, (n + 1) * RG_BLOCK)
    ra = jnp.concatenate([_dot(xcb[:, blk(n)], wa_ref[n]) for n in range(RG_BLOCKS)], axis=1)
    ri = jnp.concatenate([_dot(xcb[:, blk(n)], wi_ref[n]) for n in range(RG_BLOCKS)], axis=1)
    r = jax.nn.sigmoid(ra + ba_ref[...])
    i = jax.nn.sigmoid(ri + bi_ref[...])
    log_a = -RG_C * r * jax.nn.softplus(-lam_ref[...])
    a = jnp.exp(log_a)
    u = jnp.sqrt(-_expm1(2.0 * log_a)) * (i * xc)
    return a, u


def _expm1(x):
    e = jnp.exp(x)
    em = e - 1.0
    ok = jnp.logical_and(em != 0.0, e > 0.0)
    kahan = em * x / jnp.log(jnp.where(ok, e, 2.0))
    return jnp.where(ok, kahan, jnp.where(em == 0.0, x, em))


def _rg_kernel(xr_ref, g_ref, cw_ref, cb_ref, wa_ref, wi_ref, ba_ref, bi_ref, lam_ref,
               y_ref, conv_ref, ht_ref, xp_sc, h_sc, a_sc, u_sc, *, tt):
    t = pl.program_id(0)
    pad = 8
    d = xr_ref.shape[1]

    @pl.when(t == 0)
    def _():
        xp_sc[0:pad, :] = jnp.zeros((pad, d), F32)
        h_sc[...] = jnp.zeros(h_sc.shape, F32)

    @pl.when(t > 0)
    def _():
        xp_sc[0:pad, :] = xp_sc[tt:tt + pad, :]

    x = xr_ref[...]
    xp_sc[pad:pad + tt, :] = x
    cw = cw_ref[...]
    xc = cb_ref[...] + cw[0:1] * xp_sc[pad - 3:pad - 3 + tt, :]
    xc = xc + cw[1:2] * xp_sc[pad - 2:pad - 2 + tt, :]
    xc = xc + cw[2:3] * xp_sc[pad - 1:pad - 1 + tt, :]
    xc = xc + cw[3:4] * x
    a, u = _rg_gates(xc, wa_ref, wi_ref, ba_ref, bi_ref, lam_ref)
    a_sc[...] = a
    u_sc[...] = u

    row = lax.broadcasted_iota(jnp.int32, (8, d), 0)

    def group(gi, h):
        s0 = pl.multiple_of(gi * 8, 8)
        aa = a_sc[pl.ds(s0, 8), :]
        uu = u_sc[pl.ds(s0, 8), :]
        for sh in (1, 2, 4):
            keep = row >= sh
            uu = jnp.where(keep, uu + aa * pltpu.roll(uu, sh, 0), uu)
            aa = jnp.where(keep, aa * pltpu.roll(aa, sh, 0), aa)
        hs = aa * h + uu
        u_sc[pl.ds(s0, 8), :] = hs
        return jnp.broadcast_to(hs[7:8, :], (8, d))

    h = lax.fori_loop(0, tt // 8, group, h_sc[...])
    h_sc[...] = h
    y_ref[...] = (u_sc[...] * _silu(g_ref[...])).astype(y_ref.dtype)

    @pl.when(t == pl.num_programs(0) - 1)
    def _():
        conv_ref[...] = xp_sc[pad + tt - (CONV_W - 1):pad + tt, :]
        ht_ref[...] = h[0:1, :]


def _rg_prompt(z, conv_w, conv_b, wa, wi, ba, bi, lam, tt):
    s = z.shape[0]
    d = conv_w.shape[1]
    fix2 = lambda i: (0, 0)
    fix3 = lambda i: (0, 0, 0)
    return pl.pallas_call(
        functools.partial(_rg_kernel, tt=tt),
        grid=(s // tt,),
        in_specs=[pl.BlockSpec((tt, d), lambda i: (i, Z_XR // d)),
                  pl.BlockSpec((tt, d), lambda i: (i, Z_GRG // d)),
                  pl.BlockSpec((CONV_W, d), fix2), pl.BlockSpec((1, d), fix2),
                  pl.BlockSpec(wa.shape, fix3), pl.BlockSpec(wi.shape, fix3),
                  pl.BlockSpec((1, d), fix2), pl.BlockSpec((1, d), fix2), pl.BlockSpec((1, d), fix2)],
        out_specs=[pl.BlockSpec((tt, d), lambda i: (i, 0)),
                   pl.BlockSpec((CONV_W - 1, d), fix2), pl.BlockSpec((1, d), fix2)],
        out_shape=[jax.ShapeDtypeStruct((s, d), BF),
                   jax.ShapeDtypeStruct((CONV_W - 1, d), F32), jax.ShapeDtypeStruct((1, d), F32)],
        scratch_shapes=[pltpu.VMEM((tt + 8, d), F32), pltpu.VMEM((8, d), F32),
                        pltpu.VMEM((tt, d), F32), pltpu.VMEM((tt, d), F32)],
        compiler_params=_cp("arbitrary"),
        name="rg_prompt",
    )(z, z, conv_w, conv_b.reshape(1, d), wa, wi, ba.reshape(1, d), bi.reshape(1, d), lam.reshape(1, d))


def _rg_sample_kernel(xr_ref, g_ref, sc_ref, h0_ref, cw_ref, cb_ref, wa_ref, wi_ref, ba_ref, bi_ref,
                      lam_ref, y_ref, conv_ref, h_ref):
    d = xr_ref.shape[1]
    x = xr_ref[...]
    cw = cw_ref[...]
    xc = cb_ref[...] + cw[0:1] * sc_ref[:, 0:d]
    xc = xc + cw[1:2] * sc_ref[:, d:2 * d]
    xc = xc + cw[2:3] * sc_ref[:, 2 * d:3 * d]
    xc = xc + cw[3:4] * x
    a, u = _rg_gates(xc, wa_ref, wi_ref, ba_ref, bi_ref, lam_ref)
    h = a * h0_ref[...] + u
    h_ref[...] = h
    y_ref[...] = (h * _silu(g_ref[...])).astype(y_ref.dtype)
    conv_ref[:, 0:d] = sc_ref[:, d:2 * d]
    conv_ref[:, d:2 * d] = sc_ref[:, 2 * d:3 * d]
    conv_ref[:, 2 * d:3 * d] = x


def _rg_sample(z, state_conv, h0, conv_w, conv_b, wa, wi, ba, bi, lam):
    b = z.shape[0]
    d = conv_w.shape[1]
    nprev = CONV_W - 1
    fix2 = lambda i: (0, 0)
    fix3 = lambda i: (0, 0, 0)
    return pl.pallas_call(
        _rg_sample_kernel,
        grid=(1,),
        in_specs=[pl.BlockSpec((b, d), lambda i: (0, Z_XR // d)),
                  pl.BlockSpec((b, d), lambda i: (0, Z_GRG // d)),
                  pl.BlockSpec((b, nprev * d), fix2), pl.BlockSpec((b, d), fix2),
                  pl.BlockSpec((CONV_W, d), fix2), pl.BlockSpec((1, d), fix2),
                  pl.BlockSpec(wa.shape, fix3), pl.BlockSpec(wi.shape, fix3),
                  pl.BlockSpec((1, d), fix2), pl.BlockSpec((1, d), fix2), pl.BlockSpec((1, d), fix2)],
        out_specs=[pl.BlockSpec((b, d), fix2), pl.BlockSpec((b, nprev * d), fix2),
                   pl.BlockSpec((b, d), fix2)],
        out_shape=[jax.ShapeDtypeStruct((b, d), BF), jax.ShapeDtypeStruct((b, nprev * d), F32),
                   jax.ShapeDtypeStruct((b, d), F32)],
        compiler_params=_cp("arbitrary"),
        name="rg_sample",
    )(z, z, state_conv.reshape(b, nprev * d), h0, conv_w, conv_b.reshape(1, d), wa, wi,
      ba.reshape(1, d), bi.reshape(1, d), lam.reshape(1, d))


def _memattn_kernel(q_ref, g_ref, k_ref, v_ref, o_ref):
    scale = MEM_DIM ** -0.5
    for h in range(MEM_HEADS):
        hs = slice(h * MEM_DIM, (h + 1) * MEM_DIM)
        s = _dot_t(q_ref[:, hs].astype(BF), k_ref[:, hs].astype(BF)) * scale
        p = jnp.exp(s - jnp.max(s, axis=1, keepdims=True))
        o = _dot(p.astype(BF), v_ref[:, hs].astype(BF)) / jnp.sum(p, axis=1, keepdims=True)
        o_ref[:, hs] = (o * _silu(g_ref[:, hs])).astype(o_ref.dtype)


def _memattn(z, k, v, tm):
    s = z.shape[0]
    d = MEM_HEADS * MEM_DIM
    nm = k.shape[0]
    return pl.pallas_call(
        _memattn_kernel,
        grid=(s // tm,),
        in_specs=[pl.BlockSpec((tm, d), lambda i: (i, Z_QMEM // d)),
                  pl.BlockSpec((tm, d), lambda i: (i, Z_GMEM // d)),
                  pl.BlockSpec((nm, d), lambda i: (0, 0)), pl.BlockSpec((nm, d), lambda i: (0, 0))],
        out_specs=pl.BlockSpec((tm, d), lambda i: (i, 0)),
        out_shape=jax.ShapeDtypeStruct((s, d), BF),
        compiler_params=_cp("parallel"),
        name="memattn",
    )(z, z, k, v)


def _memattn_s_kernel(q_ref, g_ref, k_ref, v_ref, o_ref, *, bt):
    scale = MEM_DIM ** -0.5
    for b in range(bt):
        s = jnp.sum(k_ref[b] * q_ref[b][None], axis=-1, keepdims=True) * scale
        p = jnp.exp(s - jnp.max(s, axis=0, keepdims=True))
        o = jnp.sum(p * v_ref[b], axis=0) / jnp.sum(p, axis=0)
        o_ref[b] = o * _silu(g_ref[b])


def _memattn_s(q, g, k, v, bt):
    b, nm, nh, d = k.shape
    small = pl.BlockSpec((bt, nh, d), lambda i: (i, 0, 0))
    big = pl.BlockSpec((bt, nm, nh, d), lambda i: (i, 0, 0, 0))
    return pl.pallas_call(
        functools.partial(_memattn_s_kernel, bt=bt),
        grid=(b // bt,),
        in_specs=[small, small, big, big],
        out_specs=small,
        out_shape=jax.ShapeDtypeStruct((b, nh, d), F32),
        compiler_params=_cp("parallel"),
        name="memattn_s",
    )(q, g, k, v)


def _merge1_kernel(a_ref, b_ref, c_ref, wa_ref, wb_ref, wc_ref, ma_ref, mb_ref, mc_ref, o_ref):
    sg = jax.nn.sigmoid
    o = sg(ma_ref[...]) * _dot(a_ref[...].astype(BF), wa_ref[...])
    o = o + sg(mb_ref[...]) * _dot(b_ref[...].astype(BF), wb_ref[...])
    o = o + sg(mc_ref[...]) * _dot(c_ref[...].astype(BF), wc_ref[...])
    o_ref[...] = o.astype(o_ref.dtype)


def _merge1(a, b, c, wa, wb, wc, z, tm, tn):
    m = a.shape[0]
    n = wa.shape[1]
    row = lambda i, j: (i, 0)
    col = lambda i, j: (0, j)
    zcol = lambda off: (lambda i, j: (i, off // tn + j))
    return pl.pallas_call(
        _merge1_kernel,
        grid=(m // tm, n // tn),
        in_specs=[pl.BlockSpec((tm, a.shape[1]), row), pl.BlockSpec((tm, b.shape[1]), row),
                  pl.BlockSpec((tm, c.shape[1]), row),
                  pl.BlockSpec((wa.shape[0], tn), col), pl.BlockSpec((wb.shape[0], tn), col),
                  pl.BlockSpec((wc.shape[0], tn), col),
                  pl.BlockSpec((tm, tn), zcol(Z_MRG)), pl.BlockSpec((tm, tn), zcol(Z_MMLA)),
                  pl.BlockSpec((tm, tn), zcol(Z_MMEM))],
        out_specs=pl.BlockSpec((tm, tn), lambda i, j: (i, j)),
        out_shape=jax.ShapeDtypeStruct((m, n), BF),
        compiler_params=_cp("parallel", "parallel"),
        name="merge1",
    )(a, b, c, wa, wb, wc, z, z, z)


def _merge2_kernel(z_ref, w_ref, x_ref, g_ref, o_ref):
    o_ref[...] = _rms(x_ref[...] + _dot(z_ref[...], w_ref[...]), g_ref[...])


def _merge2(zz, w, x, g, tm):
    m, d = x.shape
    return pl.pallas_call(
        _merge2_kernel,
        grid=(m // tm,),
        in_specs=[pl.BlockSpec((tm, d), lambda i: (i, 0)), pl.BlockSpec(w.shape, lambda i: (0, 0)),
                  pl.BlockSpec((tm, d), lambda i: (i, 0)), pl.BlockSpec((1, d), lambda i: (0, 0))],
        out_specs=pl.BlockSpec((tm, d), lambda i: (i, 0)),
        out_shape=jax.ShapeDtypeStruct((m, d), F32),
        compiler_params=_cp("parallel"),
        name="merge2",
    )(zz, w, x, g.reshape(1, d))


def _rope_tables(pos):
    inv = ROPE_THETA ** (-jnp.arange(0, QK_ROPE, 2, dtype=F32) / QK_ROPE)
    ang = pos.astype(F32)[:, None] * inv[None, :]
    cos, sin = jnp.cos(ang), jnp.sin(ang)
    zero = jnp.zeros((pos.shape[0], LANES - QK_ROPE), F32)
    return (jnp.concatenate([cos, cos, zero], axis=1),
            jnp.concatenate([-sin, sin, zero], axis=1))


def kernel(x_prompt, x_sample, mem_prompt, cache_ckv, cache_krope, cache_mem_k, cache_mem_v, state_conv, state_rglru, page_table, norm_in, w_in, conv_w, conv_b, rg_wa, rg_ba, rg_wi, rg_bi, rg_lambda, kv_norm, w_uk, w_uv, mem_norm, w_mk, w_mv, w_rg_o, w_mla_o, w_mem_o, w_out, final_norm):
    bp, seq, d_model = x_prompt.shape
    bd, dec_seq, _ = x_sample.shape
    assert bp == 1 and dec_seq == 1
    d_rnn = conv_w.shape[1]
    n_pages = page_table.shape[1]
    past_len = n_pages * PAGE_SIZE
    half = QK_ROPE // 2

    d_q = N_HEADS * (QK_NOPE + QK_ROPE)
    d_v = N_HEADS * V_DIM
    d_m = MEM_HEADS * MEM_DIM
    o = [0]
    for sz in (d_rnn, d_rnn, d_q, KV_RANK, QK_ROPE, d_v, d_m, d_m, d_model, d_model, d_model):
        o.append(o[-1] + sz)
    assert (o[2], o[3], o[5]) == (W_Q_ROW, W_KV_ROW, W_GMLA_ROW) and o[-1] - o[5] + o[2] == Z_COLS
    wt = jnp.transpose(w_in)
    wuk2 = w_uk.reshape(KV_RANK, N_HEADS * QK_NOPE).astype(BF)
    wuv2 = w_uv.reshape(KV_RANK, d_v).astype(BF)
    wuk_t = jnp.transpose(w_uk, (1, 2, 0)).astype(BF)
    wuv_t = jnp.transpose(w_uv, (1, 0, 2)).astype(BF)
    w_rg_o_b, w_mla_o_b, w_mem_o_b, w_out_b = (w.astype(BF) for w in (w_rg_o, w_mla_o, w_mem_o, w_out))
    wa_b, wi_b = rg_wa.astype(BF), rg_wi.astype(BF)
    w_mkv = jnp.concatenate([w_mk, w_mv], axis=1).astype(BF)

    xp = x_prompt.reshape(seq, d_model)
    xs = x_sample.reshape(bd, d_model)

    cos_s, sin_s = _rope_tables(jnp.full((bd,), past_len))
    xn_s = _norm_cast(xs, norm_in, bd)
    z_s = _inproj(xn_s, wt, bd, 1024, "inproj_s")
    q_s = _qproj(xn_s, wt, cos_s, sin_s, bd)
    ckv_s, kr_s, ckvb_s, krb_s = _kvproj(xn_s, wt, cos_s, sin_s, kv_norm, bd)
    qcat = jnp.transpose(_qlat(q_s, wuk_t), (1, 0, 2))

    cos_p, sin_p = _rope_tables(jnp.arange(seq))
    xn_p = _norm_cast(xp, norm_in, 512)
    q_p, lat = _qproj_decode(xn_p, wt, cos_p, sin_p, page_table, qcat, ckvb_s, krb_s, cache_ckv,
                             jnp.transpose(cache_krope, (0, 2, 1)), 32)
    z_p = _inproj(xn_p, wt, 1024, 1024, "inproj")
    ckv_p, kr_p, ckvb_p, krb_p = _kvproj(xn_p, wt, cos_p, sin_p, kv_norm, 512)
    k_p, v_p = _kvup(ckvb_p, krb_p, wuk2, wuv2, 512)
    b_p = _flash(q_p, k_p, v_p, z_p, 512)
    a_p, conv_p, h_p = _rg_prompt(z_p, conv_w, conv_b, wa_b, wi_b, rg_ba, rg_bi, rg_lambda, 512)
    mn = _norm_cast(mem_prompt.reshape(-1, d_model), mem_norm, 256)
    mkv = _matmul(mn, w_mkv, 256, 1024, F32, "memkv")
    mem_k, mem_v = mkv[:, :d_m], mkv[:, d_m:]
    c_p = _memattn(z_p, mem_k, mem_v, 512)
    zz_p = _merge1(a_p, b_p, c_p, w_rg_o_b, w_mla_o_b, w_mem_o_b, z_p, 1024, 512)
    y_p = _merge2(zz_p, w_out_b, xp, final_norm, 512)

    b_s = _uvproj(lat, wuv_t, z_s)
    a_s, conv_s, h_s = _rg_sample(z_s, state_conv, state_rglru, conv_w, conv_b, wa_b, wi_b,
                                  rg_ba, rg_bi, rg_lambda)
    heads = lambda off: z_s[:, off:off + d_m].reshape(bd, MEM_HEADS, MEM_DIM)
    c_s = _memattn_s(heads(Z_QMEM), heads(Z_GMEM), cache_mem_k, cache_mem_v, 8).reshape(bd, d_m)
    zz_s = _merge1(a_s, b_s, c_s, w_rg_o_b, w_mla_o_b, w_mem_o_b, z_s, bd, 512)
    y_s = _merge2(zz_s, w_out_b, xs, final_norm, bd)

    n_mem = mem_prompt.shape[1]
    return (y_p.reshape(1, seq, d_model), y_s.reshape(bd, 1, d_model),
            ckv_p.reshape(1, seq, KV_RANK), kr_p.reshape(1, seq, QK_ROPE),
            conv_p.reshape(1, CONV_W - 1, d_rnn), h_p.reshape(1, d_rnn),
            mem_k.reshape(1, n_mem, MEM_HEADS, MEM_DIM), mem_v.reshape(1, n_mem, MEM_HEADS, MEM_DIM),
            ckv_s.reshape(bd, 1, KV_RANK), kr_s.reshape(bd, 1, QK_ROPE),
            conv_s.reshape(bd, CONV_W - 1, d_rnn), h_s)
```

```python
import functools
import math

import jax
import jax.numpy as jnp
from jax import lax
from jax.experimental import pallas as pl
from jax.experimental.pallas import tpu as pltpu

F32 = jnp.float32
BF = jnp.bfloat16

EPS = 1e-6
RG_BLOCKS = 8
RG_BLOCK = 128
CONV_W = 4
RG_C = 8.0
N_HEADS = 16
QK_NOPE = 128
QK_ROPE = 64
V_DIM = 128
KV_RANK = 512
ROPE_THETA = 10000.0
MEM_HEADS = 4
MEM_DIM = 256
PAGE_SIZE = 128
LANES = 128
HEAD_PAD = 256
LAT_PAD = KV_RANK + LANES
VMEM_LIMIT = 48 * 1024 * 1024
DECODE_VMEM_LIMIT = 56 * 1024 * 1024
QK_SCALE2 = (QK_NOPE + QK_ROPE) ** -0.5 * math.log2(math.e)

Z_XR, Z_GRG, Z_GMLA, Z_QMEM, Z_GMEM, Z_MRG, Z_MMLA, Z_MMEM, Z_COLS = (
    0, 1024, 2048, 4096, 5120, 6144, 8192, 10240, 12288)
W_Q_ROW, W_KV_ROW, W_GMLA_ROW = 2048, 5120, 5696
ROW_ALIGN = 64
DECODE_SLOTS = 4


def _cp(*sem):
    return pltpu.CompilerParams(dimension_semantics=sem, vmem_limit_bytes=VMEM_LIMIT)


def _silu(g):
    return g * jax.nn.sigmoid(g)


def _rms(x, g):
    return x * lax.rsqrt(jnp.mean(x * x, axis=-1, keepdims=True) + EPS) * g


def _dot(a, b):
    return jnp.dot(a, b, preferred_element_type=F32)


def _dot_t(a, b):
    return lax.dot_general(a, b, (((1,), (1,)), ((), ())), preferred_element_type=F32)


def _norm_cast_kernel(x_ref, g_ref, o_ref):
    o_ref[...] = _rms(x_ref[...], g_ref[...]).astype(o_ref.dtype)


def _norm_cast(x, g, tm):
    m, d = x.shape
    return pl.pallas_call(
        _norm_cast_kernel,
        grid=(m // tm,),
        in_specs=[pl.BlockSpec((tm, d), lambda i: (i, 0)),
                  pl.BlockSpec((1, d), lambda i: (0, 0))],
        out_specs=pl.BlockSpec((tm, d), lambda i: (i, 0)),
        out_shape=jax.ShapeDtypeStruct((m, d), BF),
        compiler_params=_cp("parallel"),
        name="norm_cast",
    )(x, g.reshape(1, d))


def _mm_kernel(a_ref, w_ref, o_ref):
    o_ref[...] = _dot(a_ref[...], w_ref[...]).astype(o_ref.dtype)


def _matmul(a, w, tm, tn, out_dtype, name):
    m, k = a.shape
    n = w.shape[1]
    return pl.pallas_call(
        _mm_kernel,
        grid=(m // tm, n // tn),
        in_specs=[pl.BlockSpec((tm, k), lambda i, j: (i, 0)),
                  pl.BlockSpec((k, tn), lambda i, j: (0, j))],
        out_specs=pl.BlockSpec((tm, tn), lambda i, j: (i, j)),
        out_shape=jax.ShapeDtypeStruct((m, n), out_dtype),
        compiler_params=_cp("parallel", "parallel"),
        name=name,
    )(a, w)


def _inproj_kernel(a_ref, as_ref, w_ref, o_ref, os_ref, wb_sc):
    @pl.when(pl.program_id(1) == 0)
    def _():
        wb_sc[...] = w_ref[...].astype(BF)
        os_ref[...] = _dot_t(as_ref[...], wb_sc[...])

    o_ref[...] = _dot_t(a_ref[...], wb_sc[...])


def _inproj(xn, xn_s, wt, tm, tn):
    m, k = xn.shape
    ms = xn_s.shape[0]
    assert W_Q_ROW % tn == 0
    skipped = W_GMLA_ROW - W_Q_ROW
    assert skipped % ROW_ALIGN == 0 and tn % ROW_ALIGN == 0
    wrow = lambda j, i: (pl.multiple_of(j * tn + jnp.where(j * tn < W_Q_ROW, 0, skipped), ROW_ALIGN), 0)
    return pl.pallas_call(
        _inproj_kernel,
        grid=(Z_COLS // tn, m // tm),
        in_specs=[pl.BlockSpec((tm, k), lambda j, i: (i, 0)),
                  pl.BlockSpec((ms, k), lambda j, i: (0, 0)),
                  pl.BlockSpec((pl.Element(tn), pl.Element(k)), wrow)],
        out_specs=[pl.BlockSpec((tm, tn), lambda j, i: (i, j)),
                   pl.BlockSpec((ms, tn), lambda j, i: (0, j))],
        out_shape=[jax.ShapeDtypeStruct((m, Z_COLS), F32), jax.ShapeDtypeStruct((ms, Z_COLS), F32)],
        scratch_shapes=[pltpu.VMEM((tn, k), BF)],
        compiler_params=_cp("parallel", "arbitrary"),
        name="inproj",
    )(xn, xn_s, wt)


def _rope_hi(hi, c, s):
    half = QK_ROPE // 2
    lane = lax.broadcasted_iota(jnp.int32, hi.shape, 1)
    swapped = jnp.where(lane < half, pltpu.roll(hi, LANES - half, 1), pltpu.roll(hi, half, 1))
    return hi * c + swapped * s


def _qproj_kernel(a_ref, w_ref, c_ref, s_ref, o_ref):
    res = _dot_t(a_ref[...], w_ref[...].astype(BF)) * QK_SCALE2
    o_ref[0, :, :QK_NOPE] = res[:, :QK_NOPE].astype(BF)
    o_ref[0, :, QK_NOPE:] = _rope_hi(res[:, QK_NOPE:], c_ref[...], s_ref[...]).astype(BF)


def _qproj(xn, wt, cos_t, sin_t, tm):
    m, d = xn.shape
    return pl.pallas_call(
        _qproj_kernel,
        grid=(m // tm, N_HEADS),
        in_specs=[pl.BlockSpec((tm, d), lambda i, h: (i, 0)),
                  pl.BlockSpec((pl.Element(HEAD_PAD), pl.Element(d)),
                               lambda i, h: (pl.multiple_of(W_Q_ROW + h * (QK_NOPE + QK_ROPE), ROW_ALIGN), 0)),
                  pl.BlockSpec((tm, LANES), lambda i, h: (i, 0)),
                  pl.BlockSpec((tm, LANES), lambda i, h: (i, 0))],
        out_specs=pl.BlockSpec((1, tm, HEAD_PAD), lambda i, h: (h, i, 0)),
        out_shape=jax.ShapeDtypeStruct((N_HEADS, m, HEAD_PAD), BF),
        compiler_params=_cp("parallel", "parallel"),
        name="qproj",
    )(xn, wt, cos_t, sin_t)


def _kvproj_kernel(a_ref, w_ref, c_ref, s_ref, g_ref, ckv_ref, kr_ref, ckvb_ref, krb_ref, wb_sc):
    @pl.when(pl.program_id(0) == 0)
    def _():
        wb_sc[...] = w_ref[...].astype(BF)

    res = _dot_t(a_ref[...], wb_sc[...])
    ckv = _rms(res[:, :KV_RANK], g_ref[...])
    ckv_ref[...] = ckv
    ckvb_ref[...] = ckv.astype(BF)
    rot = _rope_hi(res[:, KV_RANK:], c_ref[...], s_ref[...])
    kr_ref[...] = rot[:, :QK_ROPE]
    krb_ref[...] = rot.astype(BF)


def _kvproj(xn, wt, cos_t, sin_t, kv_norm, tm):
    m, d = xn.shape
    row = lambda i: (i, 0)
    fix = lambda i: (0, 0)
    return pl.pallas_call(
        _kvproj_kernel,
        grid=(m // tm,),
        in_specs=[pl.BlockSpec((tm, d), row),
                  pl.BlockSpec((pl.Element(KV_RANK + LANES), pl.Element(d)), lambda i: (W_KV_ROW, 0)),
                  pl.BlockSpec((tm, LANES), row), pl.BlockSpec((tm, LANES), row),
                  pl.BlockSpec((1, KV_RANK), fix)],
        out_specs=[pl.BlockSpec((tm, KV_RANK), row), pl.BlockSpec((tm, QK_ROPE), row),
                   pl.BlockSpec((tm, KV_RANK), row), pl.BlockSpec((tm, LANES), row)],
        out_shape=[jax.ShapeDtypeStruct((m, KV_RANK), F32), jax.ShapeDtypeStruct((m, QK_ROPE), F32),
                   jax.ShapeDtypeStruct((m, KV_RANK), BF), jax.ShapeDtypeStruct((m, LANES), BF)],
        scratch_shapes=[pltpu.VMEM((KV_RANK + LANES, d), BF)],
        compiler_params=_cp("arbitrary"),
        name="kvproj",
    )(xn, wt, cos_t, sin_t, kv_norm.reshape(1, KV_RANK))


def _kvup_kernel(c_ref, kr_ref, wk_ref, wv_ref, k_ref, v_ref):
    c = c_ref[...]
    kn = _dot(c, wk_ref[...])
    v = _dot(c, wv_ref[...])
    kr = kr_ref[...]
    ones = jnp.ones((c.shape[0], V_DIM), BF)
    for h in range(N_HEADS):
        k_ref[h, :, :QK_NOPE] = kn[:, h * QK_NOPE:(h + 1) * QK_NOPE].astype(BF)
        k_ref[h, :, QK_NOPE:] = kr
        v_ref[h, :, :V_DIM] = v[:, h * V_DIM:(h + 1) * V_DIM].astype(BF)
        v_ref[h, :, V_DIM:] = ones


def _kvup(ckvb, krb, wuk, wuv, tm):
    s = ckvb.shape[0]
    row = lambda i: (i, 0)
    fix = lambda i: (0, 0)
    return pl.pallas_call(
        _kvup_kernel,
        grid=(s // tm,),
        in_specs=[pl.BlockSpec((tm, KV_RANK), row), pl.BlockSpec((tm, LANES), row),
                  pl.BlockSpec(wuk.shape, fix), pl.BlockSpec(wuv.shape, fix)],
        out_specs=[pl.BlockSpec((N_HEADS, tm, HEAD_PAD), lambda i: (0, i, 0)),
                   pl.BlockSpec((N_HEADS, tm, 2 * V_DIM), lambda i: (0, i, 0))],
        out_shape=[jax.ShapeDtypeStruct((N_HEADS, s, HEAD_PAD), BF),
                   jax.ShapeDtypeStruct((N_HEADS, s, 2 * V_DIM), BF)],
        compiler_params=_cp("parallel"),
        name="kvup",
    )(ckvb, krb, wuk, wuv)


def _flash_kernel(q_ref, k_ref, v_ref, g_ref, o_ref, sa, sb, mxa, mxb, m_sc, acc_sc, *, tq):
    qi = pl.program_id(1)
    q = q_ref[0]
    m_sc[...] = jnp.full(m_sc.shape, -jnp.inf, F32)
    acc_sc[...] = jnp.zeros(acc_sc.shape, F32)

    def stage_x(t, s_buf, mx_buf):
        start = pl.multiple_of(t * tq, tq)
        s = _dot_t(q, k_ref[0, pl.ds(start, tq), :])
        s_buf[...] = s
        mx_buf[...] = jnp.broadcast_to(jnp.max(s, axis=1, keepdims=True), mx_buf.shape)

    def stage_y(t, s_buf, mx_buf, masked):
        s = s_buf[...]
        if masked:
            row = lax.broadcasted_iota(jnp.int32, (tq, tq), 0)
            col = lax.broadcasted_iota(jnp.int32, (tq, tq), 1)
            s = jnp.where(col <= row, s, -jnp.inf)
            mx = jnp.max(s, axis=1, keepdims=True)
        else:
            mx = mx_buf[...]
        m_prev = m_sc[...]
        m_new = jnp.maximum(m_prev, mx)
        m_sc[...] = m_new
        p = jnp.exp2(s - jnp.tile(m_new, (1, tq // LANES))).astype(BF)
        alpha = jnp.exp2(m_prev - m_new)
        start = pl.multiple_of(t * tq, tq)
        acc_sc[...] = jnp.tile(alpha, (1, 2)) * acc_sc[...] + _dot(p, v_ref[0, pl.ds(start, tq), :])

    stage_x(0, sa, mxa)

    def pair(t):
        stage_x(t + 1, sb, mxb)
        stage_y(t, sa, mxa, False)
        stage_x(t + 2, sa, mxa)
        stage_y(t + 1, sb, mxb, False)

    def quad(i, carry):
        pair(4 * i)
        pair(4 * i + 2)
        return carry

    lax.fori_loop(0, qi // 4, quad, 0)

    @pl.when(qi % 4 >= 2)
    def _():
        pair(4 * (qi // 4))

    @pl.when(qi % 2 == 1)
    def _():
        stage_x(qi, sb, mxb)
        stage_y(qi - 1, sa, mxa, False)
        stage_y(qi, sb, mxb, True)

    @pl.when(qi % 2 == 0)
    def _():
        stage_y(qi, sa, mxa, True)

    o = acc_sc[:, :V_DIM] / acc_sc[:, V_DIM:]
    o_ref[...] = (o * _silu(g_ref[...])).astype(o_ref.dtype)


def _flash(q, k, v, z, tq):
    h, s, _ = q.shape
    gcol = Z_GMLA // V_DIM
    return pl.pallas_call(
        functools.partial(_flash_kernel, tq=tq),
        grid=(h, s // tq),
        in_specs=[pl.BlockSpec((1, tq, HEAD_PAD), lambda hh, i: (hh, i, 0)),
                  pl.BlockSpec((1, s, HEAD_PAD), lambda hh, i: (hh, 0, 0)),
                  pl.BlockSpec((1, s, 2 * V_DIM), lambda hh, i: (hh, 0, 0)),
                  pl.BlockSpec((tq, V_DIM), lambda hh, i: (i, gcol + hh))],
        out_specs=pl.BlockSpec((tq, V_DIM), lambda hh, i: (i, hh)),
        out_shape=jax.ShapeDtypeStruct((s, h * V_DIM), BF),
        scratch_shapes=[pltpu.VMEM((tq, tq), F32), pltpu.VMEM((tq, tq), F32),
                        pltpu.VMEM((tq, LANES), F32), pltpu.VMEM((tq, LANES), F32),
                        pltpu.VMEM((tq, LANES), F32), pltpu.VMEM((tq, 2 * V_DIM), F32)],
        compiler_params=_cp("parallel", "arbitrary"),
        name="flash",
    )(q, k, v, z)


def _qproj_decode_kernel(pt_ref, a_ref, w_ref, c_ref, s_ref, q_ref, cn_ref, kn_ref, ckv_hbm, kr_hbm,
                         qo_ref, o_ref, cbuf, rbuf, sem, *, npg, nchunk):
    b = pl.program_id(0) * pl.num_programs(1) + pl.program_id(1)
    nb = pl.num_programs(0) * pl.num_programs(1)
    _decode_step(b, nb, pt_ref, q_ref, cn_ref, kn_ref, ckv_hbm, kr_hbm, o_ref, cbuf, rbuf, sem, npg, nchunk)
    _qproj_kernel(a_ref, w_ref, c_ref, s_ref, qo_ref)


def _decode_step(b, nb, pt_ref, q_ref, cn_ref, kn_ref, ckv_hbm, kr_hbm, o_ref, cbuf, rbuf, sem, npg, nchunk):
    nslot = cbuf.shape[0]
    ahead = nslot - 1
    def copies(seq, j, slot, wait_only=False):
        cps = []
        for p in range(npg):
            page = 0 if wait_only else pt_ref[(seq * nchunk + j) * npg + p]
            keys = pl.ds(p * PAGE_SIZE, PAGE_SIZE)
            cps.append(pltpu.make_async_copy(ckv_hbm.at[page], cbuf.at[slot, keys, :], sem.at[0, slot]))
            cps.append(pltpu.make_async_copy(kr_hbm.at[page], rbuf.at[slot, :, keys], sem.at[1, slot]))
        return cps

    @pl.when(b == 0)
    def _():
        for j in range(ahead):
            for cp in copies(0, j, j):
                cp.start()

    q = q_ref[0]
    ql = q[:, :KV_RANK]
    qr = q[:, KV_RANK:KV_RANK + QK_ROPE]
    m = jnp.full((N_HEADS, 1), -jnp.inf, F32)
    l = jnp.zeros((N_HEADS, 1), F32)
    acc = jnp.zeros((N_HEADS, KV_RANK), F32)
    for j in range(nchunk):
        slot = j % nslot
        jn = j + ahead
        if jn < nchunk:
            for cp in copies(b, jn, jn % nslot):
                cp.start()
        else:
            @pl.when(b + 1 < nb)
            def _(jn=jn):
                for cp in copies(b + 1, jn - nchunk, (jn - nchunk) % nslot):
                    cp.start()
        for cp in copies(b, j, slot, wait_only=True):
            cp.wait()
        half = npg * PAGE_SIZE // 2
        cb0 = cbuf[slot, :half, :].astype(BF)
        cb1 = cbuf[slot, half:, :].astype(BF)
        s0 = _dot_t(ql, cb0) + _dot(qr, rbuf[slot, :, :half].astype(BF))
        s1 = _dot_t(ql, cb1) + _dot(qr, rbuf[slot, :, half:].astype(BF))
        mx = jnp.maximum(jnp.max(s0, axis=1, keepdims=True), jnp.max(s1, axis=1, keepdims=True))
        m_new = jnp.maximum(m, mx)
        alpha = jnp.exp2(m - m_new)
        p0 = jnp.exp2(s0 - m_new)
        p1 = jnp.exp2(s1 - m_new)
        l = alpha * l + (jnp.sum(p0, axis=1, keepdims=True) + jnp.sum(p1, axis=1, keepdims=True))
        acc = alpha * acc + (_dot(p0.astype(BF), cb0) + _dot(p1.astype(BF), cb1))
        m = m_new

    qf = q.astype(F32)
    cn = cn_ref[0].astype(F32)
    kn = kn_ref[0].astype(F32)
    s_new = (jnp.sum(qf[:, :KV_RANK] * cn, axis=1, keepdims=True)
             + jnp.sum(qf[:, KV_RANK:] * kn, axis=1, keepdims=True))
    m_fin = jnp.maximum(m, s_new)
    a_fin = jnp.exp2(m - m_fin)
    p_new = jnp.exp2(s_new - m_fin)
    l_fin = a_fin * l + p_new
    out = (a_fin * acc + p_new.astype(BF).astype(F32) * cn) / l_fin
    for h in range(N_HEADS):
        o_ref[0, :, h * KV_RANK:(h + 1) * KV_RANK] = out[h:h + 1, :]


def _qproj_decode(xn, wt, cos_t, sin_t, page_table, qcat, ckvb, krb, cache_ckv, cache_krope_t, npg):
    m, d = xn.shape
    b, n_pages = page_table.shape
    nchunk = n_pages // npg
    assert nchunk * npg == n_pages and nchunk % DECODE_SLOTS == 0 and nchunk >= DECODE_SLOTS
    assert (m * N_HEADS) % b == 0
    tm = m * N_HEADS // b
    row = lambda i, h, pt: (i, 0)
    per_b = lambda i, h, pt: (i * N_HEADS + h, 0, 0)
    wrow = lambda i, h, pt: (pl.multiple_of(W_Q_ROW + h * (QK_NOPE + QK_ROPE), ROW_ALIGN), 0)
    grid_spec = pltpu.PrefetchScalarGridSpec(
        num_scalar_prefetch=1,
        grid=(m // tm, N_HEADS),
        in_specs=[pl.BlockSpec((tm, d), row),
                  pl.BlockSpec((pl.Element(HEAD_PAD), pl.Element(d)), wrow),
                  pl.BlockSpec((tm, LANES), row), pl.BlockSpec((tm, LANES), row),
                  pl.BlockSpec((1, N_HEADS, LAT_PAD), per_b),
                  pl.BlockSpec((1, 1, KV_RANK), per_b),
                  pl.BlockSpec((1, 1, LANES), per_b),
                  pl.BlockSpec(memory_space=pl.ANY),
                  pl.BlockSpec(memory_space=pl.ANY)],
        out_specs=[pl.BlockSpec((1, tm, HEAD_PAD), lambda i, h, pt: (h, i, 0)),
                   pl.BlockSpec((1, 1, N_HEADS * KV_RANK), per_b)],
        scratch_shapes=[pltpu.VMEM((DECODE_SLOTS, npg * PAGE_SIZE, KV_RANK), F32),
                        pltpu.VMEM((DECODE_SLOTS, QK_ROPE, npg * PAGE_SIZE), F32),
                        pltpu.SemaphoreType.DMA((2, DECODE_SLOTS))])
    q, lat = pl.pallas_call(
        functools.partial(_qproj_decode_kernel, npg=npg, nchunk=nchunk),
        grid_spec=grid_spec,
        out_shape=[jax.ShapeDtypeStruct((N_HEADS, m, HEAD_PAD), BF),
                   jax.ShapeDtypeStruct((b, 1, N_HEADS * KV_RANK), F32)],
        compiler_params=pltpu.CompilerParams(dimension_semantics=("arbitrary", "arbitrary"),
                                             vmem_limit_bytes=DECODE_VMEM_LIMIT),
        name="qproj_decode",
    )(page_table.reshape(-1), xn, wt, cos_t, sin_t, qcat, ckvb.reshape(b, 1, KV_RANK),
      krb.reshape(b, 1, LANES), cache_ckv, cache_krope_t)
    return q, lat.reshape(b, N_HEADS * KV_RANK)


def _qlat_kernel(q_ref, w_ref, o_ref):
    q = q_ref[0]
    o_ref[0, :, :KV_RANK] = _dot(q[:, :QK_NOPE], w_ref[0]).astype(BF)
    o_ref[0, :, KV_RANK:] = q[:, QK_NOPE:]


def _qlat(qs, wuk_t):
    h, b, _ = qs.shape
    return pl.pallas_call(
        _qlat_kernel,
        grid=(h,),
        in_specs=[pl.BlockSpec((1, b, HEAD_PAD), lambda i: (i, 0, 0)),
                  pl.BlockSpec((1, QK_NOPE, KV_RANK), lambda i: (i, 0, 0))],
        out_specs=pl.BlockSpec((1, b, LAT_PAD), lambda i: (i, 0, 0)),
        out_shape=jax.ShapeDtypeStruct((h, b, LAT_PAD), BF),
        compiler_params=_cp("parallel"),
        name="qlat",
    )(qs, wuk_t)


def _uvproj_kernel(l_ref, w_ref, g_ref, o_ref):
    o = _dot(l_ref[...].astype(BF), w_ref[0])
    o_ref[...] = (o * _silu(g_ref[...])).astype(o_ref.dtype)


def _uvproj(lat, wuv_t, z):
    b = lat.shape[0]
    gcol = Z_GMLA // V_DIM
    return pl.pallas_call(
        _uvproj_kernel,
        grid=(N_HEADS,),
        in_specs=[pl.BlockSpec((b, KV_RANK), lambda h: (0, h)),
                  pl.BlockSpec((1, KV_RANK, V_DIM), lambda h: (h, 0, 0)),
                  pl.BlockSpec((b, V_DIM), lambda h: (0, gcol + h))],
        out_specs=pl.BlockSpec((b, V_DIM), lambda h: (0, h)),
        out_shape=jax.ShapeDtypeStruct((b, N_HEADS * V_DIM), BF),
        compiler_params=_cp("parallel"),
        name="uvproj",
    )(lat, wuv_t, z)


def _rg_gates(xc, wa_ref, wi_ref, ba_ref, bi_ref, lam_ref):
    xcb = xc.astype(BF)
    blk = lambda n: slice(n * RG_BLOCK, (n + 1) * RG_BLOCK)
    ra = jnp.concatenate([_dot(xcb[:, blk(n)], wa_ref[n]) for n in range(RG_BLOCKS)], axis=1)
    ri = jnp.concatenate([_dot(xcb[:, blk(n)], wi_ref[n]) for n in range(RG_BLOCKS)], axis=1)
    r = jax.nn.sigmoid(ra + ba_ref[...])
    i = jax.nn.sigmoid(ri + bi_ref[...])
    log_a = -RG_C * r * jax.nn.softplus(-lam_ref[...])
    a = jnp.exp(log_a)
    u = jnp.sqrt(-_expm1(2.0 * log_a)) * (i * xc)
    return a, u


def _expm1(x):
    e = jnp.exp(x)
    em = e - 1.0
    ok = jnp.logical_and(em != 0.0, e > 0.0)
    kahan = em * x / jnp.log(jnp.where(ok, e, 2.0))
    return jnp.where(ok, kahan, jnp.where(em == 0.0, x, em))


def _rg_kernel(xr_ref, g_ref, cw_ref, cb_ref, wa_ref, wi_ref, ba_ref, bi_ref, lam_ref,
               y_ref, conv_ref, ht_ref, xp_sc, h_sc, a_sc, u_sc, *, tt):
    t = pl.program_id(0)
    pad = 8
    d = xr_ref.shape[1]

    @pl.when(t == 0)
    def _():
        xp_sc[0:pad, :] = jnp.zeros((pad, d), F32)
        h_sc[...] = jnp.zeros(h_sc.shape, F32)

    @pl.when(t > 0)
    def _():
        xp_sc[0:pad, :] = xp_sc[tt:tt + pad, :]

    x = xr_ref[...]
    xp_sc[pad:pad + tt, :] = x
    cw = cw_ref[...]
    xc = cb_ref[...] + cw[0:1] * xp_sc[pad - 3:pad - 3 + tt, :]
    xc = xc + cw[1:2] * xp_sc[pad - 2:pad - 2 + tt, :]
    xc = xc + cw[2:3] * xp_sc[pad - 1:pad - 1 + tt, :]
    xc = xc + cw[3:4] * x
    a, u = _rg_gates(xc, wa_ref, wi_ref, ba_ref, bi_ref, lam_ref)
    a_sc[...] = a
    u_sc[...] = u

    row = lax.broadcasted_iota(jnp.int32, (8, d), 0)

    def group(gi, h):
        s0 = pl.multiple_of(gi * 8, 8)
        aa = a_sc[pl.ds(s0, 8), :]
        uu = u_sc[pl.ds(s0, 8), :]
        for sh in (1, 2, 4):
            keep = row >= sh
            uu = jnp.where(keep, uu + aa * pltpu.roll(uu, sh, 0), uu)
            aa = jnp.where(keep, aa * pltpu.roll(aa, sh, 0), aa)
        hs = aa * h + uu
        u_sc[pl.ds(s0, 8), :] = hs
        return jnp.broadcast_to(hs[7:8, :], (8, d))

    h = lax.fori_loop(0, tt // 8, group, h_sc[...])
    h_sc[...] = h
    y_ref[...] = (u_sc[...] * _silu(g_ref[...])).astype(y_ref.dtype)

    @pl.when(t == pl.num_programs(0) - 1)
    def _():
        conv_ref[...] = xp_sc[pad + tt - (CONV_W - 1):pad + tt, :]
        ht_ref[...] = h[0:1, :]


def _rg_prompt(z, conv_w, conv_b, wa, wi, ba, bi, lam, tt):
    s = z.shape[0]
    d = conv_w.shape[1]
    fix2 = lambda i: (0, 0)
    fix3 = lambda i: (0, 0, 0)
    return pl.pallas_call(
        functools.partial(_rg_kernel, tt=tt),
        grid=(s // tt,),
        in_specs=[pl.BlockSpec((tt, d), lambda i: (i, Z_XR // d)),
                  pl.BlockSpec((tt, d), lambda i: (i, Z_GRG // d)),
                  pl.BlockSpec((CONV_W, d), fix2), pl.BlockSpec((1, d), fix2),
                  pl.BlockSpec(wa.shape, fix3), pl.BlockSpec(wi.shape, fix3),
                  pl.BlockSpec((1, d), fix2), pl.BlockSpec((1, d), fix2), pl.BlockSpec((1, d), fix2)],
        out_specs=[pl.BlockSpec((tt, d), lambda i: (i, 0)),
                   pl.BlockSpec((CONV_W - 1, d), fix2), pl.BlockSpec((1, d), fix2)],
        out_shape=[jax.ShapeDtypeStruct((s, d), BF),
                   jax.ShapeDtypeStruct((CONV_W - 1, d), F32), jax.ShapeDtypeStruct((1, d), F32)],
        scratch_shapes=[pltpu.VMEM((tt + 8, d), F32), pltpu.VMEM((8, d), F32),
                        pltpu.VMEM((tt, d), F32), pltpu.VMEM((tt, d), F32)],
        compiler_params=_cp("arbitrary"),
        name="rg_prompt",
    )(z, z, conv_w, conv_b.reshape(1, d), wa, wi, ba.reshape(1, d), bi.reshape(1, d), lam.reshape(1, d))


def _rg_sample_kernel(xr_ref, g_ref, sc_ref, h0_ref, cw_ref, cb_ref, wa_ref, wi_ref, ba_ref, bi_ref,
                      lam_ref, y_ref, conv_ref, h_ref):
    d = xr_ref.shape[1]
    x = xr_ref[...]
    cw = cw_ref[...]
    xc = cb_ref[...] + cw[0:1] * sc_ref[:, 0:d]
    xc = xc + cw[1:2] * sc_ref[:, d:2 * d]
    xc = xc + cw[2:3] * sc_ref[:, 2 * d:3 * d]
    xc = xc + cw[3:4] * x
    a, u = _rg_gates(xc, wa_ref, wi_ref, ba_ref, bi_ref, lam_ref)
    h = a * h0_ref[...] + u
    h_ref[...] = h
    y_ref[...] = (h * _silu(g_ref[...])).astype(y_ref.dtype)
    conv_ref[:, 0:d] = sc_ref[:, d:2 * d]
    conv_ref[:, d:2 * d] = sc_ref[:, 2 * d:3 * d]
    conv_ref[:, 2 * d:3 * d] = x


def _rg_sample(z, state_conv, h0, conv_w, conv_b, wa, wi, ba, bi, lam):
    b = z.shape[0]
    d = conv_w.shape[1]
    nprev = CONV_W - 1
    fix2 = lambda i: (0, 0)
    fix3 = lambda i: (0, 0, 0)
    return pl.pallas_call(
        _rg_sample_kernel,
        grid=(1,),
        in_specs=[pl.BlockSpec((b, d), lambda i: (0, Z_XR // d)),
                  pl.BlockSpec((b, d), lambda i: (0, Z_GRG // d)),
                  pl.BlockSpec((b, nprev * d), fix2), pl.BlockSpec((b, d), fix2),
                  pl.BlockSpec((CONV_W, d), fix2), pl.BlockSpec((1, d), fix2),
                  pl.BlockSpec(wa.shape, fix3), pl.BlockSpec(wi.shape, fix3),
                  pl.BlockSpec((1, d), fix2), pl.BlockSpec((1, d), fix2), pl.BlockSpec((1, d), fix2)],
        out_specs=[pl.BlockSpec((b, d), fix2), pl.BlockSpec((b, nprev * d), fix2),
                   pl.BlockSpec((b, d), fix2)],
        out_shape=[jax.ShapeDtypeStruct((b, d), BF), jax.ShapeDtypeStruct((b, nprev * d), F32),
                   jax.ShapeDtypeStruct((b, d), F32)],
        compiler_params=_cp("arbitrary"),
        name="rg_sample",
    )(z, z, state_conv.reshape(b, nprev * d), h0, conv_w, conv_b.reshape(1, d), wa, wi,
      ba.reshape(1, d), bi.reshape(1, d), lam.reshape(1, d))


def _memattn_kernel(q_ref, g_ref, k_ref, v_ref, o_ref):
    scale = MEM_DIM ** -0.5
    for h in range(MEM_HEADS):
        hs = slice(h * MEM_DIM, (h + 1) * MEM_DIM)
        s = _dot_t(q_ref[:, hs].astype(BF), k_ref[:, hs].astype(BF)) * scale
        p = jnp.exp(s - jnp.max(s, axis=1, keepdims=True))
        o = _dot(p.astype(BF), v_ref[:, hs].astype(BF)) / jnp.sum(p, axis=1, keepdims=True)
        o_ref[:, hs] = (o * _silu(g_ref[:, hs])).astype(o_ref.dtype)


def _memattn(z, k, v, tm):
    s = z.shape[0]
    d = MEM_HEADS * MEM_DIM
    nm = k.shape[0]
    return pl.pallas_call(
        _memattn_kernel,
        grid=(s // tm,),
        in_specs=[pl.BlockSpec((tm, d), lambda i: (i, Z_QMEM // d)),
                  pl.BlockSpec((tm, d), lambda i: (i, Z_GMEM // d)),
                  pl.BlockSpec((nm, d), lambda i: (0, 0)), pl.BlockSpec((nm, d), lambda i: (0, 0))],
        out_specs=pl.BlockSpec((tm, d), lambda i: (i, 0)),
        out_shape=jax.ShapeDtypeStruct((s, d), BF),
        compiler_params=_cp("parallel"),
        name="memattn",
    )(z, z, k, v)


def _memattn_s_kernel(q_ref, g_ref, k_ref, v_ref, o_ref, *, bt):
    scale = MEM_DIM ** -0.5
    for b in range(bt):
        s = jnp.sum(k_ref[b] * q_ref[b][None], axis=-1, keepdims=True) * scale
        p = jnp.exp(s - jnp.max(s, axis=0, keepdims=True))
        o = jnp.sum(p * v_ref[b], axis=0) / jnp.sum(p, axis=0)
        o_ref[b] = o * _silu(g_ref[b])


def _memattn_s(q, g, k, v, bt):
    b, nm, nh, d = k.shape
    small = pl.BlockSpec((bt, nh, d), lambda i: (i, 0, 0))
    big = pl.BlockSpec((bt, nm, nh, d), lambda i: (i, 0, 0, 0))
    return pl.pallas_call(
        functools.partial(_memattn_s_kernel, bt=bt),
        grid=(b // bt,),
        in_specs=[small, small, big, big],
        out_specs=small,
        out_shape=jax.ShapeDtypeStruct((b, nh, d), F32),
        compiler_params=_cp("parallel"),
        name="memattn_s",
    )(q, g, k, v)


def _merge1_kernel(a_ref, b_ref, c_ref, wa_ref, wb_ref, wc_ref, ma_ref, mb_ref, mc_ref, o_ref):
    sg = jax.nn.sigmoid
    o = sg(ma_ref[...]) * _dot(a_ref[...].astype(BF), wa_ref[...])
    o = o + sg(mb_ref[...]) * _dot(b_ref[...].astype(BF), wb_ref[...])
    o = o + sg(mc_ref[...]) * _dot(c_ref[...].astype(BF), wc_ref[...])
    o_ref[...] = o.astype(o_ref.dtype)


def _merge1(a, b, c, wa, wb, wc, z, tm, tn):
    m = a.shape[0]
    n = wa.shape[1]
    row = lambda i, j: (i, 0)
    col = lambda i, j: (0, j)
    zcol = lambda off: (lambda i, j: (i, off // tn + j))
    return pl.pallas_call(
        _merge1_kernel,
        grid=(m // tm, n // tn),
        in_specs=[pl.BlockSpec((tm, a.shape[1]), row), pl.BlockSpec((tm, b.shape[1]), row),
                  pl.BlockSpec((tm, c.shape[1]), row),
                  pl.BlockSpec((wa.shape[0], tn), col), pl.BlockSpec((wb.shape[0], tn), col),
                  pl.BlockSpec((wc.shape[0], tn), col),
                  pl.BlockSpec((tm, tn), zcol(Z_MRG)), pl.BlockSpec((tm, tn), zcol(Z_MMLA)),
                  pl.BlockSpec((tm, tn), zcol(Z_MMEM))],
        out_specs=pl.BlockSpec((tm, tn), lambda i, j: (i, j)),
        out_shape=jax.ShapeDtypeStruct((m, n), BF),
        compiler_params=_cp("parallel", "parallel"),
        name="merge1",
    )(a, b, c, wa, wb, wc, z, z, z)


def _merge2_kernel(z_ref, w_ref, x_ref, g_ref, o_ref):
    o_ref[...] = _rms(x_ref[...] + _dot(z_ref[...], w_ref[...]), g_ref[...])


def _merge2(zz, w, x, g, tm):
    m, d = x.shape
    return pl.pallas_call(
        _merge2_kernel,
        grid=(m // tm,),
        in_specs=[pl.BlockSpec((tm, d), lambda i: (i, 0)), pl.BlockSpec(w.shape, lambda i: (0, 0)),
                  pl.BlockSpec((tm, d), lambda i: (i, 0)), pl.BlockSpec((1, d), lambda i: (0, 0))],
        out_specs=pl.BlockSpec((tm, d), lambda i: (i, 0)),
        out_shape=jax.ShapeDtypeStruct((m, d), F32),
        compiler_params=_cp("parallel"),
        name="merge2",
    )(zz, w, x, g.reshape(1, d))


def _rope_tables(pos):
    inv = ROPE_THETA ** (-jnp.arange(0, QK_ROPE, 2, dtype=F32) / QK_ROPE)
    ang = pos.astype(F32)[:, None] * inv[None, :]
    cos, sin = jnp.cos(ang), jnp.sin(ang)
    zero = jnp.zeros((pos.shape[0], LANES - QK_ROPE), F32)
    return (jnp.concatenate([cos, cos, zero], axis=1),
            jnp.concatenate([-sin, sin, zero], axis=1))


def kernel(x_prompt, x_sample, mem_prompt, cache_ckv, cache_krope, cache_mem_k, cache_mem_v, state_conv, state_rglru, page_table, norm_in, w_in, conv_w, conv_b, rg_wa, rg_ba, rg_wi, rg_bi, rg_lambda, kv_norm, w_uk, w_uv, mem_norm, w_mk, w_mv, w_rg_o, w_mla_o, w_mem_o, w_out, final_norm):
    bp, seq, d_model = x_prompt.shape
    bd, dec_seq, _ = x_sample.shape
    assert bp == 1 and dec_seq == 1
    d_rnn = conv_w.shape[1]
    n_pages = page_table.shape[1]
    past_len = n_pages * PAGE_SIZE
    half = QK_ROPE // 2

    d_q = N_HEADS * (QK_NOPE + QK_ROPE)
    d_v = N_HEADS * V_DIM
    d_m = MEM_HEADS * MEM_DIM
    o = [0]
    for sz in (d_rnn, d_rnn, d_q, KV_RANK, QK_ROPE, d_v, d_m, d_m, d_model, d_model, d_model):
        o.append(o[-1] + sz)
    assert (o[2], o[3], o[5]) == (W_Q_ROW, W_KV_ROW, W_GMLA_ROW) and o[-1] - o[5] + o[2] == Z_COLS
    wt = jnp.transpose(w_in)
    wuk2 = w_uk.reshape(KV_RANK, N_HEADS * QK_NOPE).astype(BF)
    wuv2 = w_uv.reshape(KV_RANK, d_v).astype(BF)
    wuk_t = jnp.transpose(w_uk, (1, 2, 0)).astype(BF)
    wuv_t = jnp.transpose(w_uv, (1, 0, 2)).astype(BF)
    w_rg_o_b, w_mla_o_b, w_mem_o_b, w_out_b = (w.astype(BF) for w in (w_rg_o, w_mla_o, w_mem_o, w_out))
    wa_b, wi_b = rg_wa.astype(BF), rg_wi.astype(BF)
    w_mkv = jnp.concatenate([w_mk, w_mv], axis=1).astype(BF)

    xp = x_prompt.reshape(seq, d_model)
    xs = x_sample.reshape(bd, d_model)

    cos_s, sin_s = _rope_tables(jnp.full((bd,), past_len))
    xn_s = _norm_cast(xs, norm_in, bd)
    q_s = _qproj(xn_s, wt, cos_s, sin_s, bd)
    ckv_s, kr_s, ckvb_s, krb_s = _kvproj(xn_s, wt, cos_s, sin_s, kv_norm, bd)
    qcat = jnp.transpose(_qlat(q_s, wuk_t), (1, 0, 2))

    cos_p, sin_p = _rope_tables(jnp.arange(seq))
    xn_p = _norm_cast(xp, norm_in, 512)
    q_p, lat = _qproj_decode(xn_p, wt, cos_p, sin_p, page_table, qcat, ckvb_s, krb_s, cache_ckv,
                             jnp.transpose(cache_krope, (0, 2, 1)), 32)
    z_p, z_s = _inproj(xn_p, xn_s, wt, 1024, 1024)
    ckv_p, kr_p, ckvb_p, krb_p = _kvproj(xn_p, wt, cos_p, sin_p, kv_norm, 512)
    k_p, v_p = _kvup(ckvb_p, krb_p, wuk2, wuv2, 512)
    b_p = _flash(q_p, k_p, v_p, z_p, 512)
    a_p, conv_p, h_p = _rg_prompt(z_p, conv_w, conv_b, wa_b, wi_b, rg_ba, rg_bi, rg_lambda, 512)
    mn = _norm_cast(mem_prompt.reshape(-1, d_model), mem_norm, 256)
    mkv = _matmul(mn, w_mkv, 256, 1024, F32, "memkv")
    mem_k, mem_v = mkv[:, :d_m], mkv[:, d_m:]
    c_p = _memattn(z_p, mem_k, mem_v, 512)
    zz_p = _merge1(a_p, b_p, c_p, w_rg_o_b, w_mla_o_b, w_mem_o_b, z_p, 1024, 512)
    y_p = _merge2(zz_p, w_out_b, xp, final_norm, 512)

    b_s = _uvproj(lat, wuv_t, z_s)
    a_s, conv_s, h_s = _rg_sample(z_s, state_conv, state_rglru, conv_w, conv_b, wa_b, wi_b,
                                  rg_ba, rg_bi, rg_lambda)
    heads = lambda off: z_s[:, off:off + d_m].reshape(bd, MEM_HEADS, MEM_DIM)
    c_s = _memattn_s(heads(Z_QMEM), heads(Z_GMEM), cache_mem_k, cache_mem_v, 8).reshape(bd, d_m)
    zz_s = _merge1(a_s, b_s, c_s, w_rg_o_b, w_mla_o_b, w_mem_o_b, z_s, bd, 512)
    y_s = _merge2(zz_s, w_out_b, xs, final_norm, bd)

    n_mem = mem_prompt.shape[1]
    return (y_p.reshape(1, seq, d_model), y_s.reshape(bd, 1, d_model),
            ckv_p.reshape(1, seq, KV_RANK), kr_p.reshape(1, seq, QK_ROPE),
            conv_p.reshape(1, CONV_W - 1, d_rnn), h_p.reshape(1, d_rnn),
            mem_k.reshape(1, n_mem, MEM_HEADS, MEM_DIM), mem_v.reshape(1, n_mem, MEM_HEADS, MEM_DIM),
            ckv_s.reshape(bd, 1, KV_RANK), kr_s.reshape(bd, 1, QK_ROPE),
            conv_s.reshape(bd, CONV_W - 1, d_rnn), h_s)
```

```python
import functools
import math

import jax
import jax.numpy as jnp
from jax import lax
from jax.experimental import pallas as pl
from jax.experimental.pallas import tpu as pltpu

F32 = jnp.float32
BF = jnp.bfloat16

EPS = 1e-6
RG_BLOCKS = 8
RG_BLOCK = 128
CONV_W = 4
RG_C = 8.0
N_HEADS = 16
QK_NOPE = 128
QK_ROPE = 64
V_DIM = 128
KV_RANK = 512
ROPE_THETA = 10000.0
MEM_HEADS = 4
MEM_DIM = 256
PAGE_SIZE = 128
LANES = 128
HEAD_PAD = 256
LAT_PAD = KV_RANK + LANES
VMEM_LIMIT = 48 * 1024 * 1024
DECODE_VMEM_LIMIT = 56 * 1024 * 1024
QK_SCALE2 = (QK_NOPE + QK_ROPE) ** -0.5 * math.log2(math.e)

Z_XR, Z_GRG, Z_GMLA, Z_QMEM, Z_GMEM, Z_MRG, Z_MMLA, Z_MMEM, Z_COLS = (
    0, 1024, 2048, 4096, 5120, 6144, 8192, 10240, 12288)
W_Q_ROW, W_KV_ROW, W_GMLA_ROW = 2048, 5120, 5696
ROW_ALIGN = 64
DECODE_SLOTS = 4


def _cp(*sem):
    return pltpu.CompilerParams(dimension_semantics=sem, vmem_limit_bytes=VMEM_LIMIT)


def _silu(g):
    return g * jax.nn.sigmoid(g)


def _rms(x, g):
    return x * lax.rsqrt(jnp.mean(x * x, axis=-1, keepdims=True) + EPS) * g


def _dot(a, b):
    return jnp.dot(a, b, preferred_element_type=F32)


def _dot_t(a, b):
    return lax.dot_general(a, b, (((1,), (1,)), ((), ())), preferred_element_type=F32)


def _norm_cast_kernel(x_ref, g_ref, o_ref):
    o_ref[...] = _rms(x_ref[...], g_ref[...]).astype(o_ref.dtype)


def _norm_cast(x, g, tm):
    m, d = x.shape
    return pl.pallas_call(
        _norm_cast_kernel,
        grid=(m // tm,),
        in_specs=[pl.BlockSpec((tm, d), lambda i: (i, 0)),
                  pl.BlockSpec((1, d), lambda i: (0, 0))],
        out_specs=pl.BlockSpec((tm, d), lambda i: (i, 0)),
        out_shape=jax.ShapeDtypeStruct((m, d), BF),
        compiler_params=_cp("parallel"),
        name="norm_cast",
    )(x, g.reshape(1, d))


def _mm_kernel(a_ref, w_ref, o_ref):
    o_ref[...] = _dot(a_ref[...], w_ref[...]).astype(o_ref.dtype)


def _matmul(a, w, tm, tn, out_dtype, name):
    m, k = a.shape
    n = w.shape[1]
    return pl.pallas_call(
        _mm_kernel,
        grid=(m // tm, n // tn),
        in_specs=[pl.BlockSpec((tm, k), lambda i, j: (i, 0)),
                  pl.BlockSpec((k, tn), lambda i, j: (0, j))],
        out_specs=pl.BlockSpec((tm, tn), lambda i, j: (i, j)),
        out_shape=jax.ShapeDtypeStruct((m, n), out_dtype),
        compiler_params=_cp("parallel", "parallel"),
        name=name,
    )(a, w)


def _inproj_kernel(a_ref, as_ref, w_ref, o_ref, os_ref, wb_sc):
    @pl.when(pl.program_id(1) == 0)
    def _():
        wb_sc[...] = w_ref[...].astype(BF)
        os_ref[...] = _dot_t(as_ref[...], wb_sc[...])

    o_ref[...] = _dot_t(a_ref[...], wb_sc[...])


def _inproj(xn, xn_s, wt, tm, tn):
    m, k = xn.shape
    ms = xn_s.shape[0]
    assert W_Q_ROW % tn == 0
    skipped = W_GMLA_ROW - W_Q_ROW
    assert skipped % ROW_ALIGN == 0 and tn % ROW_ALIGN == 0
    wrow = lambda j, i: (pl.multiple_of(j * tn + jnp.where(j * tn < W_Q_ROW, 0, skipped), ROW_ALIGN), 0)
    return pl.pallas_call(
        _inproj_kernel,
        grid=(Z_COLS // tn, m // tm),
        in_specs=[pl.BlockSpec((tm, k), lambda j, i: (i, 0)),
                  pl.BlockSpec((ms, k), lambda j, i: (0, 0)),
                  pl.BlockSpec((pl.Element(tn), pl.Element(k)), wrow)],
        out_specs=[pl.BlockSpec((tm, tn), lambda j, i: (i, j)),
                   pl.BlockSpec((ms, tn), lambda j, i: (0, j))],
        out_shape=[jax.ShapeDtypeStruct((m, Z_COLS), F32), jax.ShapeDtypeStruct((ms, Z_COLS), F32)],
        scratch_shapes=[pltpu.VMEM((tn, k), BF)],
        compiler_params=_cp("parallel", "arbitrary"),
        name="inproj",
    )(xn, xn_s, wt)


def _rope_hi(hi, c, s):
    half = QK_ROPE // 2
    lane = lax.broadcasted_iota(jnp.int32, hi.shape, 1)
    swapped = jnp.where(lane < half, pltpu.roll(hi, LANES - half, 1), pltpu.roll(hi, half, 1))
    return hi * c + swapped * s


def _qproj_kernel(a_ref, w_ref, c_ref, s_ref, o_ref):
    res = _dot_t(a_ref[...], w_ref[...].astype(BF)) * QK_SCALE2
    o_ref[0, :, :QK_NOPE] = res[:, :QK_NOPE].astype(BF)
    o_ref[0, :, QK_NOPE:] = _rope_hi(res[:, QK_NOPE:], c_ref[...], s_ref[...]).astype(BF)


def _qproj(xn, wt, cos_t, sin_t, tm):
    m, d = xn.shape
    return pl.pallas_call(
        _qproj_kernel,
        grid=(m // tm, N_HEADS),
        in_specs=[pl.BlockSpec((tm, d), lambda i, h: (i, 0)),
                  pl.BlockSpec((pl.Element(HEAD_PAD), pl.Element(d)),
                               lambda i, h: (pl.multiple_of(W_Q_ROW + h * (QK_NOPE + QK_ROPE), ROW_ALIGN), 0)),
                  pl.BlockSpec((tm, LANES), lambda i, h: (i, 0)),
                  pl.BlockSpec((tm, LANES), lambda i, h: (i, 0))],
        out_specs=pl.BlockSpec((1, tm, HEAD_PAD), lambda i, h: (h, i, 0)),
        out_shape=jax.ShapeDtypeStruct((N_HEADS, m, HEAD_PAD), BF),
        compiler_params=_cp("parallel", "parallel"),
        name="qproj",
    )(xn, wt, cos_t, sin_t)


def _kvproj_kernel(a_ref, w_ref, c_ref, s_ref, g_ref, ckv_ref, kr_ref, ckvb_ref, krb_ref, wb_sc):
    @pl.when(pl.program_id(0) == 0)
    def _():
        wb_sc[...] = w_ref[...].astype(BF)

    res = _dot_t(a_ref[...], wb_sc[...])
    ckv = _rms(res[:, :KV_RANK], g_ref[...])
    ckv_ref[...] = ckv
    ckvb_ref[...] = ckv.astype(BF)
    rot = _rope_hi(res[:, KV_RANK:], c_ref[...], s_ref[...])
    kr_ref[...] = rot[:, :QK_ROPE]
    krb_ref[...] = rot.astype(BF)


def _kvproj(xn, wt, cos_t, sin_t, kv_norm, tm):
    m, d = xn.shape
    row = lambda i: (i, 0)
    fix = lambda i: (0, 0)
    return pl.pallas_call(
        _kvproj_kernel,
        grid=(m // tm,),
        in_specs=[pl.BlockSpec((tm, d), row),
                  pl.BlockSpec((pl.Element(KV_RANK + LANES), pl.Element(d)), lambda i: (W_KV_ROW, 0)),
                  pl.BlockSpec((tm, LANES), row), pl.BlockSpec((tm, LANES), row),
                  pl.BlockSpec((1, KV_RANK), fix)],
        out_specs=[pl.BlockSpec((tm, KV_RANK), row), pl.BlockSpec((tm, QK_ROPE), row),
                   pl.BlockSpec((tm, KV_RANK), row), pl.BlockSpec((tm, LANES), row)],
        out_shape=[jax.ShapeDtypeStruct((m, KV_RANK), F32), jax.ShapeDtypeStruct((m, QK_ROPE), F32),
                   jax.ShapeDtypeStruct((m, KV_RANK), BF), jax.ShapeDtypeStruct((m, LANES), BF)],
        scratch_shapes=[pltpu.VMEM((KV_RANK + LANES, d), BF)],
        compiler_params=_cp("arbitrary"),
        name="kvproj",
    )(xn, wt, cos_t, sin_t, kv_norm.reshape(1, KV_RANK))


def _kvup_kernel(c_ref, kr_ref, wk_ref, wv_ref, k_ref, v_ref):
    c = c_ref[...]
    kn = _dot(c, wk_ref[...])
    v = _dot(c, wv_ref[...])
    kr = kr_ref[...]
    ones = jnp.ones((c.shape[0], V_DIM), BF)
    for h in range(N_HEADS):
        k_ref[h, :, :QK_NOPE] = kn[:, h * QK_NOPE:(h + 1) * QK_NOPE].astype(BF)
        k_ref[h, :, QK_NOPE:] = kr
        v_ref[h, :, :V_DIM] = v[:, h * V_DIM:(h + 1) * V_DIM].astype(BF)
        v_ref[h, :, V_DIM:] = ones


def _kvup(ckvb, krb, wuk, wuv, tm):
    s = ckvb.shape[0]
    row = lambda i: (i, 0)
    fix = lambda i: (0, 0)
    return pl.pallas_call(
        _kvup_kernel,
        grid=(s // tm,),
        in_specs=[pl.BlockSpec((tm, KV_RANK), row), pl.BlockSpec((tm, LANES), row),
                  pl.BlockSpec(wuk.shape, fix), pl.BlockSpec(wuv.shape, fix)],
        out_specs=[pl.BlockSpec((N_HEADS, tm, HEAD_PAD), lambda i: (0, i, 0)),
                   pl.BlockSpec((N_HEADS, tm, 2 * V_DIM), lambda i: (0, i, 0))],
        out_shape=[jax.ShapeDtypeStruct((N_HEADS, s, HEAD_PAD), BF),
                   jax.ShapeDtypeStruct((N_HEADS, s, 2 * V_DIM), BF)],
        compiler_params=_cp("parallel"),
        name="kvup",
    )(ckvb, krb, wuk, wuv)


def _flash_kernel(q_ref, k_ref, v_ref, g_ref, o_ref, sa, sb, mxa, mxb, m_sc, acc_sc, *, tq):
    qi = pl.program_id(1)
    q = q_ref[0]
    m_sc[...] = jnp.full(m_sc.shape, -jnp.inf, F32)
    acc_sc[...] = jnp.zeros(acc_sc.shape, F32)

    def stage_x(t, s_buf, mx_buf):
        start = pl.multiple_of(t * tq, tq)
        s = _dot_t(q, k_ref[0, pl.ds(start, tq), :])
        s_buf[...] = s
        mx_buf[...] = jnp.broadcast_to(jnp.max(s, axis=1, keepdims=True), mx_buf.shape)

    def stage_y(t, s_buf, mx_buf, masked):
        s = s_buf[...]
        if masked:
            row = lax.broadcasted_iota(jnp.int32, (tq, tq), 0)
            col = lax.broadcasted_iota(jnp.int32, (tq, tq), 1)
            s = jnp.where(col <= row, s, -jnp.inf)
            mx = jnp.max(s, axis=1, keepdims=True)
        else:
            mx = mx_buf[...]
        m_prev = m_sc[...]
        m_new = jnp.maximum(m_prev, mx)
        m_sc[...] = m_new
        p = jnp.exp2(s - jnp.tile(m_new, (1, tq // LANES))).astype(BF)
        alpha = jnp.exp2(m_prev - m_new)
        start = pl.multiple_of(t * tq, tq)
        acc_sc[...] = jnp.tile(alpha, (1, 2)) * acc_sc[...] + _dot(p, v_ref[0, pl.ds(start, tq), :])

    stage_x(0, sa, mxa)

    def pair(t):
        stage_x(t + 1, sb, mxb)
        stage_y(t, sa, mxa, False)
        stage_x(t + 2, sa, mxa)
        stage_y(t + 1, sb, mxb, False)

    def quad(i, carry):
        pair(4 * i)
        pair(4 * i + 2)
        return carry

    lax.fori_loop(0, qi // 4, quad, 0)

    @pl.when(qi % 4 >= 2)
    def _():
        pair(4 * (qi // 4))

    @pl.when(qi % 2 == 1)
    def _():
        stage_x(qi, sb, mxb)
        stage_y(qi - 1, sa, mxa, False)
        stage_y(qi, sb, mxb, True)

    @pl.when(qi % 2 == 0)
    def _():
        stage_y(qi, sa, mxa, True)

    o = acc_sc[:, :V_DIM] / acc_sc[:, V_DIM:]
    o_ref[...] = (o * _silu(g_ref[...])).astype(o_ref.dtype)


def _flash(q, k, v, z, tq):
    h, s, _ = q.shape
    gcol = Z_GMLA // V_DIM
    return pl.pallas_call(
        functools.partial(_flash_kernel, tq=tq),
        grid=(h, s // tq),
        in_specs=[pl.BlockSpec((1, tq, HEAD_PAD), lambda hh, i: (hh, i, 0)),
                  pl.BlockSpec((1, s, HEAD_PAD), lambda hh, i: (hh, 0, 0)),
                  pl.BlockSpec((1, s, 2 * V_DIM), lambda hh, i: (hh, 0, 0)),
                  pl.BlockSpec((tq, V_DIM), lambda hh, i: (i, gcol + hh))],
        out_specs=pl.BlockSpec((tq, V_DIM), lambda hh, i: (i, hh)),
        out_shape=jax.ShapeDtypeStruct((s, h * V_DIM), BF),
        scratch_shapes=[pltpu.VMEM((tq, tq), F32), pltpu.VMEM((tq, tq), F32),
                        pltpu.VMEM((tq, LANES), F32), pltpu.VMEM((tq, LANES), F32),
                        pltpu.VMEM((tq, LANES), F32), pltpu.VMEM((tq, 2 * V_DIM), F32)],
        compiler_params=_cp("parallel", "arbitrary"),
        name="flash",
    )(q, k, v, z)


def _qproj_decode_kernel(pt_ref, a_ref, w_ref, c_ref, s_ref, q_ref, cn_ref, kn_ref, ckv_hbm, kr_hbm,
                         qo_ref, o_ref, cbuf, rbuf, sem, *, npg, nchunk):
    b = pl.program_id(0) * pl.num_programs(1) + pl.program_id(1)
    nb = pl.num_programs(0) * pl.num_programs(1)
    _decode_step(b, nb, pt_ref, q_ref, cn_ref, kn_ref, ckv_hbm, kr_hbm, o_ref, cbuf, rbuf, sem, npg, nchunk)
    _qproj_kernel(a_ref, w_ref, c_ref, s_ref, qo_ref)


def _decode_step(b, nb, pt_ref, q_ref, cn_ref, kn_ref, ckv_hbm, kr_hbm, o_ref, cbuf, rbuf, sem, npg, nchunk):
    nslot = cbuf.shape[0]
    ahead = nslot - 1
    def copies(seq, j, slot, wait_only=False):
        cps = []
        for p in range(npg):
            page = 0 if wait_only else pt_ref[(seq * nchunk + j) * npg + p]
            keys = pl.ds(p * PAGE_SIZE, PAGE_SIZE)
            cps.append(pltpu.make_async_copy(ckv_hbm.at[page], cbuf.at[slot, keys, :], sem.at[0, slot]))
            cps.append(pltpu.make_async_copy(kr_hbm.at[page], rbuf.at[slot, :, keys], sem.at[1, slot]))
        return cps

    @pl.when(b == 0)
    def _():
        for j in range(ahead):
            for cp in copies(0, j, j):
                cp.start()

    q = q_ref[0]
    ql = q[:, :KV_RANK]
    qr = q[:, KV_RANK:KV_RANK + QK_ROPE]
    m = jnp.full((N_HEADS, 1), -jnp.inf, F32)
    l = jnp.zeros((N_HEADS, 1), F32)
    acc = jnp.zeros((N_HEADS, KV_RANK), F32)
    for j in range(nchunk):
        slot = j % nslot
        jn = j + ahead
        if jn < nchunk:
            for cp in copies(b, jn, jn % nslot):
                cp.start()
        else:
            @pl.when(b + 1 < nb)
            def _(jn=jn):
                for cp in copies(b + 1, jn - nchunk, (jn - nchunk) % nslot):
                    cp.start()
        for cp in copies(b, j, slot, wait_only=True):
            cp.wait()
        half = npg * PAGE_SIZE // 2
        cb0 = cbuf[slot, :half, :].astype(BF)
        cb1 = cbuf[slot, half:, :].astype(BF)
        s0 = _dot_t(ql, cb0) + _dot(qr, rbuf[slot, :, :half].astype(BF))
        s1 = _dot_t(ql, cb1) + _dot(qr, rbuf[slot, :, half:].astype(BF))
        mx = jnp.maximum(jnp.max(s0, axis=1, keepdims=True), jnp.max(s1, axis=1, keepdims=True))
        m_new = jnp.maximum(m, mx)
        alpha = jnp.exp2(m - m_new)
        p0 = jnp.exp2(s0 - m_new)
        p1 = jnp.exp2(s1 - m_new)
        l = alpha * l + (jnp.sum(p0, axis=1, keepdims=True) + jnp.sum(p1, axis=1, keepdims=True))
        acc = alpha * acc + (_dot(p0.astype(BF), cb0) + _dot(p1.astype(BF), cb1))
        m = m_new

    qf = q.astype(F32)
    cn = cn_ref[0].astype(F32)
    kn = kn_ref[0].astype(F32)
    s_new = (jnp.sum(qf[:, :KV_RANK] * cn, axis=1, keepdims=True)
             + jnp.sum(qf[:, KV_RANK:] * kn, axis=1, keepdims=True))
    m_fin = jnp.maximum(m, s_new)
    a_fin = jnp.exp2(m - m_fin)
    p_new = jnp.exp2(s_new - m_fin)
    l_fin = a_fin * l + p_new
    out = (a_fin * acc + p_new.astype(BF).astype(F32) * cn) / l_fin
    for h in range(N_HEADS):
        o_ref[0, :, h * KV_RANK:(h + 1) * KV_RANK] = out[h:h + 1, :]


def _qproj_decode(xn, wt, cos_t, sin_t, page_table, qcat, ckvb, krb, cache_ckv, cache_krope_t, npg):
    m, d = xn.shape
    b, n_pages = page_table.shape
    nchunk = n_pages // npg
    assert nchunk * npg == n_pages and nchunk % DECODE_SLOTS == 0 and nchunk >= DECODE_SLOTS
    assert (m * N_HEADS) % b == 0
    tm = m * N_HEADS // b
    row = lambda i, h, pt: (i, 0)
    per_b = lambda i, h, pt: (i * N_HEADS + h, 0, 0)
    wrow = lambda i, h, pt: (pl.multiple_of(W_Q_ROW + h * (QK_NOPE + QK_ROPE), ROW_ALIGN), 0)
    grid_spec = pltpu.PrefetchScalarGridSpec(
        num_scalar_prefetch=1,
        grid=(m // tm, N_HEADS),
        in_specs=[pl.BlockSpec((tm, d), row),
                  pl.BlockSpec((pl.Element(HEAD_PAD), pl.Element(d)), wrow),
                  pl.BlockSpec((tm, LANES), row), pl.BlockSpec((tm, LANES), row),
                  pl.BlockSpec((1, N_HEADS, LAT_PAD), per_b),
                  pl.BlockSpec((1, 1, KV_RANK), per_b),
                  pl.BlockSpec((1, 1, LANES), per_b),
                  pl.BlockSpec(memory_space=pl.ANY),
                  pl.BlockSpec(memory_space=pl.ANY)],
        out_specs=[pl.BlockSpec((1, tm, HEAD_PAD), lambda i, h, pt: (h, i, 0)),
                   pl.BlockSpec((1, 1, N_HEADS * KV_RANK), per_b)],
        scratch_shapes=[pltpu.VMEM((DECODE_SLOTS, npg * PAGE_SIZE, KV_RANK), F32),
                        pltpu.VMEM((DECODE_SLOTS, QK_ROPE, npg * PAGE_SIZE), F32),
                        pltpu.SemaphoreType.DMA((2, DECODE_SLOTS))])
    q, lat = pl.pallas_call(
        functools.partial(_qproj_decode_kernel, npg=npg, nchunk=nchunk),
        grid_spec=grid_spec,
        out_shape=[jax.ShapeDtypeStruct((N_HEADS, m, HEAD_PAD), BF),
                   jax.ShapeDtypeStruct((b, 1, N_HEADS * KV_RANK), F32)],
        compiler_params=pltpu.CompilerParams(dimension_semantics=("arbitrary", "arbitrary"),
                                             vmem_limit_bytes=DECODE_VMEM_LIMIT),
        name="qproj_decode",
    )(page_table.reshape(-1), xn, wt, cos_t, sin_t, qcat, ckvb.reshape(b, 1, KV_RANK),
      krb.reshape(b, 1, LANES), cache_ckv, cache_krope_t)
    return q, lat.reshape(b, N_HEADS * KV_RANK)


def _qlat_kernel(q_ref, w_ref, o_ref):
    q = q_ref[0]
    o_ref[0, :, :KV_RANK] = _dot(q[:, :QK_NOPE], w_ref[0]).astype(BF)
    o_ref[0, :, KV_RANK:] = q[:, QK_NOPE:]


def _qlat(qs, wuk_t):
    h, b, _ = qs.shape
    return pl.pallas_call(
        _qlat_kernel,
        grid=(h,),
        in_specs=[pl.BlockSpec((1, b, HEAD_PAD), lambda i: (i, 0, 0)),
                  pl.BlockSpec((1, QK_NOPE, KV_RANK), lambda i: (i, 0, 0))],
        out_specs=pl.BlockSpec((1, b, LAT_PAD), lambda i: (i, 0, 0)),
        out_shape=jax.ShapeDtypeStruct((h, b, LAT_PAD), BF),
        compiler_params=_cp("parallel"),
        name="qlat",
    )(qs, wuk_t)


def _uvproj_kernel(l_ref, w_ref, g_ref, o_ref):
    o = _dot(l_ref[...].astype(BF), w_ref[0])
    o_ref[...] = (o * _silu(g_ref[...])).astype(o_ref.dtype)


def _uvproj(lat, wuv_t, z):
    b = lat.shape[0]
    gcol = Z_GMLA // V_DIM
    return pl.pallas_call(
        _uvproj_kernel,
        grid=(N_HEADS,),
        in_specs=[pl.BlockSpec((b, KV_RANK), lambda h: (0, h)),
                  pl.BlockSpec((1, KV_RANK, V_DIM), lambda h: (h, 0, 0)),
                  pl.BlockSpec((b, V_DIM), lambda h: (0, gcol + h))],
        out_specs=pl.BlockSpec((b, V_DIM), lambda h: (0, h)),
        out_shape=jax.ShapeDtypeStruct((b, N_HEADS * V_DIM), BF),
        compiler_params=_cp("parallel"),
        name="uvproj",
    )(lat, wuv_t, z)


def _rg_gates(xc, wa_ref, wi_ref, ba_ref, bi_ref, lam_ref):
    xcb = xc.astype(BF)
    blk = lambda n: slice(n * RG_BLOCK, (n + 1) * RG_BLOCK)
    ra = jnp.concatenate([_dot(xcb[:, blk(n)], wa_ref[n]) for n in range(RG_BLOCKS)], axis=1)
    ri = jnp.concatenate([_dot(xcb[:, blk(n)], wi_ref[n]) for n in range(RG_BLOCKS)], axis=1)
    r = jax.nn.sigmoid(ra + ba_ref[...])
    i = jax.nn.sigmoid(ri + bi_ref[...])
    log_a = -RG_C * r * jax.nn.softplus(-lam_ref[...])
    a = jnp.exp(log_a)
    th = jnp.tanh(log_a)
    u = jnp.sqrt(-2.0 * th / (1.0 - th)) * (i * xc)
    return a, u


def _rg_kernel(xr_ref, g_ref, cw_ref, cb_ref, wa_ref, wi_ref, ba_ref, bi_ref, lam_ref,
               y_ref, conv_ref, ht_ref, xp_sc, h_sc, a_sc, u_sc, *, tt):
    t = pl.program_id(0)
    pad = 8
    d = xr_ref.shape[1]

    @pl.when(t == 0)
    def _():
        xp_sc[0:pad, :] = jnp.zeros((pad, d), F32)
        h_sc[...] = jnp.zeros(h_sc.shape, F32)

    @pl.when(t > 0)
    def _():
        xp_sc[0:pad, :] = xp_sc[tt:tt + pad, :]

    x = xr_ref[...]
    xp_sc[pad:pad + tt, :] = x
    cw = cw_ref[...]
    xc = cb_ref[...] + cw[0:1] * xp_sc[pad - 3:pad - 3 + tt, :]
    xc = xc + cw[1:2] * xp_sc[pad - 2:pad - 2 + tt, :]
    xc = xc + cw[2:3] * xp_sc[pad - 1:pad - 1 + tt, :]
    xc = xc + cw[3:4] * x
    a, u = _rg_gates(xc, wa_ref, wi_ref, ba_ref, bi_ref, lam_ref)
    a_sc[...] = a
    u_sc[...] = u

    row = lax.broadcasted_iota(jnp.int32, (8, d), 0)

    def group(gi, h):
        s0 = pl.multiple_of(gi * 8, 8)
        aa = a_sc[pl.ds(s0, 8), :]
        uu = u_sc[pl.ds(s0, 8), :]
        for sh in (1, 2, 4):
            keep = row >= sh
            uu = jnp.where(keep, uu + aa * pltpu.roll(uu, sh, 0), uu)
            aa = jnp.where(keep, aa * pltpu.roll(aa, sh, 0), aa)
        hs = aa * h + uu
        u_sc[pl.ds(s0, 8), :] = hs
        return jnp.broadcast_to(hs[7:8, :], (8, d))

    h = lax.fori_loop(0, tt // 8, group, h_sc[...])
    h_sc[...] = h
    y_ref[...] = (u_sc[...] * _silu(g_ref[...])).astype(y_ref.dtype)

    @pl.when(t == pl.num_programs(0) - 1)
    def _():
        conv_ref[...] = xp_sc[pad + tt - (CONV_W - 1):pad + tt, :]
        ht_ref[...] = h[0:1, :]


def _rg_prompt(z, conv_w, conv_b, wa, wi, ba, bi, lam, tt):
    s = z.shape[0]
    d = conv_w.shape[1]
    fix2 = lambda i: (0, 0)
    fix3 = lambda i: (0, 0, 0)
    return pl.pallas_call(
        functools.partial(_rg_kernel, tt=tt),
        grid=(s // tt,),
        in_specs=[pl.BlockSpec((tt, d), lambda i: (i, Z_XR // d)),
                  pl.BlockSpec((tt, d), lambda i: (i, Z_GRG // d)),
                  pl.BlockSpec((CONV_W, d), fix2), pl.BlockSpec((1, d), fix2),
                  pl.BlockSpec(wa.shape, fix3), pl.BlockSpec(wi.shape, fix3),
                  pl.BlockSpec((1, d), fix2), pl.BlockSpec((1, d), fix2), pl.BlockSpec((1, d), fix2)],
        out_specs=[pl.BlockSpec((tt, d), lambda i: (i, 0)),
                   pl.BlockSpec((CONV_W - 1, d), fix2), pl.BlockSpec((1, d), fix2)],
        out_shape=[jax.ShapeDtypeStruct((s, d), BF),
                   jax.ShapeDtypeStruct((CONV_W - 1, d), F32), jax.ShapeDtypeStruct((1, d), F32)],
        scratch_shapes=[pltpu.VMEM((tt + 8, d), F32), pltpu.VMEM((8, d), F32),
                        pltpu.VMEM((tt, d), F32), pltpu.VMEM((tt, d), F32)],
        compiler_params=_cp("arbitrary"),
        name="rg_prompt",
    )(z, z, conv_w, conv_b.reshape(1, d), wa, wi, ba.reshape(1, d), bi.reshape(1, d), lam.reshape(1, d))


def _rg_sample_kernel(xr_ref, g_ref, sc_ref, h0_ref, cw_ref, cb_ref, wa_ref, wi_ref, ba_ref, bi_ref,
                      lam_ref, y_ref, conv_ref, h_ref):
    d = xr_ref.shape[1]
    x = xr_ref[...]
    cw = cw_ref[...]
    xc = cb_ref[...] + cw[0:1] * sc_ref[:, 0:d]
    xc = xc + cw[1:2] * sc_ref[:, d:2 * d]
    xc = xc + cw[2:3] * sc_ref[:, 2 * d:3 * d]
    xc = xc + cw[3:4] * x
    a, u = _rg_gates(xc, wa_ref, wi_ref, ba_ref, bi_ref, lam_ref)
    h = a * h0_ref[...] + u
    h_ref[...] = h
    y_ref[...] = (h * _silu(g_ref[...])).astype(y_ref.dtype)
    conv_ref[:, 0:d] = sc_ref[:, d:2 * d]
    conv_ref[:, d:2 * d] = sc_ref[:, 2 * d:3 * d]
    conv_ref[:, 2 * d:3 * d] = x


def _rg_sample(z, state_conv, h0, conv_w, conv_b, wa, wi, ba, bi, lam):
    b = z.shape[0]
    d = conv_w.shape[1]
    nprev = CONV_W - 1
    fix2 = lambda i: (0, 0)
    fix3 = lambda i: (0, 0, 0)
    return pl.pallas_call(
        _rg_sample_kernel,
        grid=(1,),
        in_specs=[pl.BlockSpec((b, d), lambda i: (0, Z_XR // d)),
                  pl.BlockSpec((b, d), lambda i: (0, Z_GRG // d)),
                  pl.BlockSpec((b, nprev * d), fix2), pl.BlockSpec((b, d), fix2),
                  pl.BlockSpec((CONV_W, d), fix2), pl.BlockSpec((1, d), fix2),
                  pl.BlockSpec(wa.shape, fix3), pl.BlockSpec(wi.shape, fix3),
                  pl.BlockSpec((1, d), fix2), pl.BlockSpec((1, d), fix2), pl.BlockSpec((1, d), fix2)],
        out_specs=[pl.BlockSpec((b, d), fix2), pl.BlockSpec((b, nprev * d), fix2),
                   pl.BlockSpec((b, d), fix2)],
        out_shape=[jax.ShapeDtypeStruct((b, d), BF), jax.ShapeDtypeStruct((b, nprev * d), F32),
                   jax.ShapeDtypeStruct((b, d), F32)],
        compiler_params=_cp("arbitrary"),
        name="rg_sample",
    )(z, z, state_conv.reshape(b, nprev * d), h0, conv_w, conv_b.reshape(1, d), wa, wi,
      ba.reshape(1, d), bi.reshape(1, d), lam.reshape(1, d))


def _memattn_kernel(q_ref, g_ref, k_ref, v_ref, o_ref):
    scale = MEM_DIM ** -0.5
    for h in range(MEM_HEADS):
        hs = slice(h * MEM_DIM, (h + 1) * MEM_DIM)
        s = _dot_t(q_ref[:, hs].astype(BF), k_ref[:, hs].astype(BF)) * scale
        p = jnp.exp(s - jnp.max(s, axis=1, keepdims=True))
        o = _dot(p.astype(BF), v_ref[:, hs].astype(BF)) / jnp.sum(p, axis=1, keepdims=True)
        o_ref[:, hs] = (o * _silu(g_ref[:, hs])).astype(o_ref.dtype)


def _memattn(z, k, v, tm):
    s = z.shape[0]
    d = MEM_HEADS * MEM_DIM
    nm = k.shape[0]
    return pl.pallas_call(
        _memattn_kernel,
        grid=(s // tm,),
        in_specs=[pl.BlockSpec((tm, d), lambda i: (i, Z_QMEM // d)),
                  pl.BlockSpec((tm, d), lambda i: (i, Z_GMEM // d)),
                  pl.BlockSpec((nm, d), lambda i: (0, 0)), pl.BlockSpec((nm, d), lambda i: (0, 0))],
        out_specs=pl.BlockSpec((tm, d), lambda i: (i, 0)),
        out_shape=jax.ShapeDtypeStruct((s, d), BF),
        compiler_params=_cp("parallel"),
        name="memattn",
    )(z, z, k, v)


def _memattn_s_kernel(q_ref, g_ref, k_ref, v_ref, o_ref, *, bt):
    scale = MEM_DIM ** -0.5
    for b in range(bt):
        s = jnp.sum(k_ref[b] * q_ref[b][None], axis=-1, keepdims=True) * scale
        p = jnp.exp(s - jnp.max(s, axis=0, keepdims=True))
        o = jnp.sum(p * v_ref[b], axis=0) / jnp.sum(p, axis=0)
        o_ref[b] = o * _silu(g_ref[b])


def _memattn_s(q, g, k, v, bt):
    b, nm, nh, d = k.shape
    small = pl.BlockSpec((bt, nh, d), lambda i: (i, 0, 0))
    big = pl.BlockSpec((bt, nm, nh, d), lambda i: (i, 0, 0, 0))
    return pl.pallas_call(
        functools.partial(_memattn_s_kernel, bt=bt),
        grid=(b // bt,),
        in_specs=[small, small, big, big],
        out_specs=small,
        out_shape=jax.ShapeDtypeStruct((b, nh, d), F32),
        compiler_params=_cp("parallel"),
        name="memattn_s",
    )(q, g, k, v)


def _merge1_kernel(a_ref, b_ref, c_ref, wa_ref, wb_ref, wc_ref, ma_ref, mb_ref, mc_ref, o_ref):
    sg = jax.nn.sigmoid
    o = sg(ma_ref[...]) * _dot(a_ref[...].astype(BF), wa_ref[...])
    o = o + sg(mb_ref[...]) * _dot(b_ref[...].astype(BF), wb_ref[...])
    o = o + sg(mc_ref[...]) * _dot(c_ref[...].astype(BF), wc_ref[...])
    o_ref[...] = o.astype(o_ref.dtype)


def _merge1(a, b, c, wa, wb, wc, z, tm, tn):
    m = a.shape[0]
    n = wa.shape[1]
    row = lambda i, j: (i, 0)
    col = lambda i, j: (0, j)
    zcol = lambda off: (lambda i, j: (i, off // tn + j))
    return pl.pallas_call(
        _merge1_kernel,
        grid=(m // tm, n // tn),
        in_specs=[pl.BlockSpec((tm, a.shape[1]), row), pl.BlockSpec((tm, b.shape[1]), row),
                  pl.BlockSpec((tm, c.shape[1]), row),
                  pl.BlockSpec((wa.shape[0], tn), col), pl.BlockSpec((wb.shape[0], tn), col),
                  pl.BlockSpec((wc.shape[0], tn), col),
                  pl.BlockSpec((tm, tn), zcol(Z_MRG)), pl.BlockSpec((tm, tn), zcol(Z_MMLA)),
                  pl.BlockSpec((tm, tn), zcol(Z_MMEM))],
        out_specs=pl.BlockSpec((tm, tn), lambda i, j: (i, j)),
        out_shape=jax.ShapeDtypeStruct((m, n), BF),
        compiler_params=_cp("parallel", "parallel"),
        name="merge1",
    )(a, b, c, wa, wb, wc, z, z, z)


def _merge2_kernel(z_ref, w_ref, x_ref, g_ref, o_ref):
    o_ref[...] = _rms(x_ref[...] + _dot(z_ref[...], w_ref[...]), g_ref[...])


def _merge2(zz, w, x, g, tm):
    m, d = x.shape
    return pl.pallas_call(
        _merge2_kernel,
        grid=(m // tm,),
        in_specs=[pl.BlockSpec((tm, d), lambda i: (i, 0)), pl.BlockSpec(w.shape, lambda i: (0, 0)),
                  pl.BlockSpec((tm, d), lambda i: (i, 0)), pl.BlockSpec((1, d), lambda i: (0, 0))],
        out_specs=pl.BlockSpec((tm, d), lambda i: (i, 0)),
        out_shape=jax.ShapeDtypeStruct((m, d), F32),
        compiler_params=_cp("parallel"),
        name="merge2",
    )(zz, w, x, g.reshape(1, d))


def _rope_tables(pos):
    inv = ROPE_THETA ** (-jnp.arange(0, QK_ROPE, 2, dtype=F32) / QK_ROPE)
    ang = pos.astype(F32)[:, None] * inv[None, :]
    cos, sin = jnp.cos(ang), jnp.sin(ang)
    zero = jnp.zeros((pos.shape[0], LANES - QK_ROPE), F32)
    return (jnp.concatenate([cos, cos, zero], axis=1),
            jnp.concatenate([-sin, sin, zero], axis=1))


def kernel(x_prompt, x_sample, mem_prompt, cache_ckv, cache_krope, cache_mem_k, cache_mem_v, state_conv, state_rglru, page_table, norm_in, w_in, conv_w, conv_b, rg_wa, rg_ba, rg_wi, rg_bi, rg_lambda, kv_norm, w_uk, w_uv, mem_norm, w_mk, w_mv, w_rg_o, w_mla_o, w_mem_o, w_out, final_norm):
    bp, seq, d_model = x_prompt.shape
    bd, dec_seq, _ = x_sample.shape
    assert bp == 1 and dec_seq == 1
    d_rnn = conv_w.shape[1]
    n_pages = page_table.shape[1]
    past_len = n_pages * PAGE_SIZE
    half = QK_ROPE // 2

    d_q = N_HEADS * (QK_NOPE + QK_ROPE)
    d_v = N_HEADS * V_DIM
    d_m = MEM_HEADS * MEM_DIM
    o = [0]
    for sz in (d_rnn, d_rnn, d_q, KV_RANK, QK_ROPE, d_v, d_m, d_m, d_model, d_model, d_model):
        o.append(o[-1] + sz)
    assert (o[2], o[3], o[5]) == (W_Q_ROW, W_KV_ROW, W_GMLA_ROW) and o[-1] - o[5] + o[2] == Z_COLS
    wt = jnp.transpose(w_in)
    wuk2 = w_uk.reshape(KV_RANK, N_HEADS * QK_NOPE).astype(BF)
    wuv2 = w_uv.reshape(KV_RANK, d_v).astype(BF)
    wuk_t = jnp.transpose(w_uk, (1, 2, 0)).astype(BF)
    wuv_t = jnp.transpose(w_uv, (1, 0, 2)).astype(BF)
    w_rg_o_b, w_mla_o_b, w_mem_o_b, w_out_b = (w.astype(BF) for w in (w_rg_o, w_mla_o, w_mem_o, w_out))
    wa_b, wi_b = rg_wa.astype(BF), rg_wi.astype(BF)
    w_mkv = jnp.concatenate([w_mk, w_mv], axis=1).astype(BF)

    xp = x_prompt.reshape(seq, d_model)
    xs = x_sample.reshape(bd, d_model)

    cos_s, sin_s = _rope_tables(jnp.full((bd,), past_len))
    xn_s = _norm_cast(xs, norm_in, bd)
    q_s = _qproj(xn_s, wt, cos_s, sin_s, bd)
    ckv_s, kr_s, ckvb_s, krb_s = _kvproj(xn_s, wt, cos_s, sin_s, kv_norm, bd)
    qcat = jnp.transpose(_qlat(q_s, wuk_t), (1, 0, 2))

    cos_p, sin_p = _rope_tables(jnp.arange(seq))
    xn_p = _norm_cast(xp, norm_in, 512)
    q_p, lat = _qproj_decode(xn_p, wt, cos_p, sin_p, page_table, qcat, ckvb_s, krb_s, cache_ckv,
                             jnp.transpose(cache_krope, (0, 2, 1)), 32)
    z_p, z_s = _inproj(xn_p, xn_s, wt, 1024, 1024)
    ckv_p, kr_p, ckvb_p, krb_p = _kvproj(xn_p, wt, cos_p, sin_p, kv_norm, 512)
    k_p, v_p = _kvup(ckvb_p, krb_p, wuk2, wuv2, 512)
    b_p = _flash(q_p, k_p, v_p, z_p, 512)
    a_p, conv_p, h_p = _rg_prompt(z_p, conv_w, conv_b, wa_b, wi_b, rg_ba, rg_bi, rg_lambda, 512)
    mn = _norm_cast(mem_prompt.reshape(-1, d_model), mem_norm, 256)
    mkv = _matmul(mn, w_mkv, 256, 1024, F32, "memkv")
    mem_k, mem_v = mkv[:, :d_m], mkv[:, d_m:]
    c_p = _memattn(z_p, mem_k, mem_v, 512)
    zz_p = _merge1(a_p, b_p, c_p, w_rg_o_b, w_mla_o_b, w_mem_o_b, z_p, 1024, 512)
    y_p = _merge2(zz_p, w_out_b, xp, final_norm, 512)

    b_s = _uvproj(lat, wuv_t, z_s)
    a_s, conv_s, h_s = _rg_sample(z_s, state_conv, state_rglru, conv_w, conv_b, wa_b, wi_b,
                                  rg_ba, rg_bi, rg_lambda)
    heads = lambda off: z_s[:, off:off + d_m].reshape(bd, MEM_HEADS, MEM_DIM)
    c_s = _memattn_s(heads(Z_QMEM), heads(Z_GMEM), cache_mem_k, cache_mem_v, 8).reshape(bd, d_m)
    zz_s = _merge1(a_s, b_s, c_s, w_rg_o_b, w_mla_o_b, w_mem_o_b, z_s, bd, 512)
    y_s = _merge2(zz_s, w_out_b, xs, final_norm, bd)

    n_mem = mem_prompt.shape[1]
    return (y_p.reshape(1, seq, d_model), y_s.reshape(bd, 1, d_model),
            ckv_p.reshape(1, seq, KV_RANK), kr_p.reshape(1, seq, QK_ROPE),
            conv_p.reshape(1, CONV_W - 1, d_rnn), h_p.reshape(1, d_rnn),
            mem_k.reshape(1, n_mem, MEM_HEADS, MEM_DIM), mem_v.reshape(1, n_mem, MEM_HEADS, MEM_DIM),
            ckv_s.reshape(bd, 1, KV_RANK), kr_s.reshape(bd, 1, QK_ROPE),
            conv_s.reshape(bd, CONV_W - 1, d_rnn), h_s)
```
